```python
import math
import jax, jax.numpy as jnp
from jax import lax
import numpy as np

D_MODEL = 2048
BATCH = 1
SEQ = 8192
DEPTH = 4

HA_HEADS = 6
HA_DK = 128
HA_DV = 128
GB_HEADS = 6
GB_DK = 128
GB_DV = 128
GB_CONV = 4
S5_GROUPS = 32
S5_CH = 16
S5_STATE = 64
S5_DT_MIN = 1e-3
S5_DT_MAX = 1e-1
CHUNK = 64
XA_HEADS = 4
XA_DH = 128
MEM_LEN = 256
D_FF = 5632
N_EXPERTS = 8
TOP_K = 2
MOE_BLOCK = 128
NORM_EPS = 1e-6
LB_FLOOR = 1e-30

HA_W_K = HA_HEADS * HA_DK
HA_W_V = HA_HEADS * HA_DV
GB_W_K = GB_HEADS * GB_DK
GB_W_V = GB_HEADS * GB_DV
S5_W = S5_GROUPS * S5_CH
MIX_W = HA_W_V + GB_W_V + S5_W
XA_W = XA_HEADS * XA_DH
CONV_CH = 2 * GB_W_K + GB_W_V
SPLIT_SIZES = (HA_W_K, HA_W_K, HA_W_V, HA_W_V, GB_W_K, GB_W_K, GB_W_V, GB_W_V, GB_HEADS, GB_HEADS, S5_W)
IN_COLS = 6668

kernel_name = "hymba_style_hgrn2_gdn_s5_moe_trunk"


def rms_norm(x, w):
    xf = x.astype(jnp.float32)
    y = xf * lax.rsqrt(jnp.mean(xf * xf, axis=-1, keepdims=True) + NORM_EPS)
    return (y * w.astype(jnp.float32)).astype(x.dtype)


def l2_normalize(x):
    return x * lax.rsqrt(jnp.sum(x * x, axis=-1, keepdims=True) + 1e-6)


def to_chunks(t):
    b, s, h, d = t.shape
    return t.reshape(b, s // CHUNK, CHUNK, h, d).transpose(1, 0, 3, 2, 4)


def from_chunks(t):
    n, b, h, c, d = t.shape
    return t.transpose(1, 0, 3, 2, 4).reshape(b, n * c, h, d)


def hgrn_lower_bounds(lb_logits):
    p = jax.nn.softmax(lb_logits.astype(jnp.float32), axis=0)
    return jnp.cumsum(p, axis=0) - p[0]


def gla_chunked(q, k, v, log_f):
    b, _, h, dk = q.shape
    dv = v.shape[-1]
    causal = jnp.tril(jnp.ones((CHUNK, CHUNK), bool))[:, :, None]

    def step(state, inp):
        qi, ki, vi, gi = inp
        cum = jnp.cumsum(gi, axis=-2)
        diff = cum[..., :, None, :] - cum[..., None, :, :]
        decay = jnp.where(causal, jnp.exp(jnp.where(causal, diff, 0.0)), 0.0)
        attn = jnp.einsum('bhtd,bhsd,bhtsd->bhts', qi, ki, decay)
        o = (jnp.einsum('bhtd,bhde->bhte', qi * jnp.exp(cum), state)
             + jnp.einsum('bhts,bhse->bhte', attn, vi))
        last = cum[..., -1:, :]
        state = (jnp.exp(last[..., 0, :])[..., None] * state
                 + jnp.einsum('bhsd,bhse->bhde', ki * jnp.exp(last - cum), vi))
        return state, o

    s0 = jnp.zeros((b, h, dk, dv), jnp.float32)
    _, o = lax.scan(step, s0, (to_chunks(q), to_chunks(k), to_chunks(v), to_chunks(log_f)))
    return from_chunks(o)


def gated_delta_chunked(q, k, v, g, beta):
    b, _, h, dk = q.shape
    dv = v.shape[-1]
    incl = jnp.tril(jnp.ones((CHUNK, CHUNK), bool))
    strict = jnp.tril(jnp.ones((CHUNK, CHUNK), bool), -1)
    eye = jnp.eye(CHUNK, dtype=jnp.float32)

    def step(state, inp):
        qi, ki, vi, gi, bi = inp
        cum = jnp.cumsum(gi, axis=-1)
        diff = cum[..., :, None] - cum[..., None, :]
        gam = jnp.where(incl, jnp.exp(jnp.where(incl, diff, 0.0)), 0.0)
        kk = jnp.einsum('bhtd,bhsd->bhts', ki, ki)
        m = eye + jnp.where(strict, kk * gam, 0.0) * bi[..., :, None]
        rhs = jnp.concatenate([vi * bi[..., None], ki * (bi * jnp.exp(cum))[..., None]], axis=-1)
        sol = lax.linalg.triangular_solve(m, rhs, left_side=True, lower=True, unit_diagonal=True)
        u, w = sol[..., :dv], sol[..., dv:]
        v_new = u - jnp.einsum('bhtd,bhde->bhte', w, state)
        qk = jnp.einsum('bhtd,bhsd->bhts', qi, ki) * gam
        o = (jnp.einsum('bhtd,bhde->bhte', qi * jnp.exp(cum)[..., None], state)
             + jnp.einsum('bhts,bhse->bhte', qk, v_new))
        last = cum[..., -1]
        state = (jnp.exp(last)[..., None, None] * state
                 + jnp.einsum('bhsd,bhse->bhde', ki * jnp.exp(last[..., None] - cum)[..., None], v_new))
        return state, o

    s0 = jnp.zeros((b, h, dk, dv), jnp.float32)
    gc = to_chunks(g[..., None])[..., 0]
    bc = to_chunks(beta[..., None])[..., 0]
    _, o = lax.scan(step, s0, (to_chunks(q), to_chunks(k), to_chunks(v), gc, bc))
    return from_chunks(o)


def causal_depthwise_conv(x, w):
    kw = w.shape[0]
    return lax.conv_general_dilated(x, w[:, None, :], window_strides=(1,), padding=[(kw - 1, 0)],
                                    dimension_numbers=('NWC', 'WIO', 'NWC'),
                                    feature_group_count=x.shape[-1])


def s5_mixer(u, a_re, a_im, b_re, b_im, c_re, c_im, d_skip, log_step, w_glu, b_glu):
    f32 = jnp.float32
    bsz, s, _ = u.shape
    u = u.astype(f32).reshape(bsz, s, S5_GROUPS, S5_CH)
    lam = lax.complex(jnp.minimum(a_re.astype(f32), -1e-4), a_im.astype(f32))
    delta = jnp.exp(log_step.astype(f32))[:, None]
    lam_bar = jnp.exp(lam * delta)
    b_bar = ((lam_bar - 1.0) / lam)[..., None] * lax.complex(b_re.astype(f32), b_im.astype(f32))
    bu = jnp.einsum('btgc,gnc->btgn', u.astype(jnp.complex64), b_bar)
    lam_seq = jnp.broadcast_to(lam_bar, bu.shape)

    def combine(left, right):
        a1, x1 = left
        a2, x2 = right
        return a2 * a1, a2 * x1 + x2

    _, states = lax.associative_scan(combine, (lam_seq, bu), axis=1)
    c = lax.complex(c_re.astype(f32), c_im.astype(f32))
    y = jnp.real(jnp.einsum('btgn,gcn->btgc', states, c)) + d_skip.astype(f32) * u
    y = jax.nn.gelu(y.reshape(bsz, s, S5_W))
    return y * jax.nn.sigmoid(y @ w_glu.astype(f32) + b_glu.astype(f32))


def hybrid_mixer(h, w_in, w_out, lb, hgrn_norm, gdn_conv, gdn_a_log, gdn_dt_bias, gdn_norm,
                 s5_a_re, s5_a_im, s5_b_re, s5_b_im, s5_c_re, s5_c_im, s5_d, s5_log_step, s5_w_glu, s5_b_glu):
    f32 = jnp.float32
    bsz, s, _ = h.shape
    proj = h @ w_in
    split_at = [int(v) for v in np.cumsum(SPLIT_SIZES)[:-1]]
    qa, fa, ia, ga, qb, kb, vb, zb, ab, bb, uc = jnp.split(proj, split_at, axis=-1)

    def heads(t, nh):
        return t.astype(f32).reshape(bsz, s, nh, -1)

    lbh = lb.astype(f32).reshape(HA_HEADS, HA_DK)
    fa = heads(fa, HA_HEADS)
    log_f = jnp.logaddexp(jnp.log(jnp.maximum(lbh, LB_FLOOR)), jnp.log1p(-lbh) + jax.nn.log_sigmoid(fa))
    ka = (1.0 - lbh) * jax.nn.sigmoid(-fa)
    qa = jax.nn.silu(heads(qa, HA_HEADS)) * HA_DK ** -0.5
    oa = gla_chunked(qa, ka, heads(ia, HA_HEADS), log_f)
    oa = rms_norm(oa, hgrn_norm) * jax.nn.silu(heads(ga, HA_HEADS))

    qkv = jax.nn.silu(causal_depthwise_conv(jnp.concatenate([qb, kb, vb], axis=-1), gdn_conv))
    qb, kb, vb = jnp.split(qkv, [GB_W_K, 2 * GB_W_K], axis=-1)
    qb = l2_normalize(heads(qb, GB_HEADS)) * GB_DK ** -0.5
    kb = l2_normalize(heads(kb, GB_HEADS))
    beta = jax.nn.sigmoid(bb.astype(f32))
    g = -jnp.exp(gdn_a_log.astype(f32)) * jax.nn.softplus(ab.astype(f32) + gdn_dt_bias.astype(f32))
    ob = gated_delta_chunked(qb, kb, heads(vb, GB_HEADS), g, beta)
    ob = rms_norm(ob, gdn_norm) * jax.nn.silu(heads(zb, GB_HEADS))

    yc = s5_mixer(uc, s5_a_re, s5_a_im, s5_b_re, s5_b_im, s5_c_re, s5_c_im, s5_d, s5_log_step,
                  s5_w_glu, s5_b_glu)

    mixed = jnp.concatenate([oa.reshape(bsz, s, HA_W_V), ob.reshape(bsz, s, GB_W_V), yc], axis=-1)
    return mixed.astype(h.dtype) @ w_out


def cross_attention(h, mem_n, wq, wk, wv, wo):
    bsz, s, _ = h.shape
    m = mem_n.shape[1]
    q = (h @ wq).reshape(bsz, s, XA_HEADS, XA_DH)
    k = (mem_n @ wk).reshape(bsz, m, XA_HEADS, XA_DH)
    v = (mem_n @ wv).reshape(bsz, m, XA_HEADS, XA_DH)
    scores = jnp.einsum('bqhd,bkhd->bhqk', q, k).astype(jnp.float32) * XA_DH ** -0.5
    p = jax.nn.softmax(scores, axis=-1).astype(v.dtype)
    o = jnp.einsum('bhqk,bkhd->bqhd', p, v).reshape(bsz, s, XA_W)
    return o @ wo


def swiglu(x, w_gate, w_up, w_down):
    return (jax.nn.silu(x @ w_gate) * (x @ w_up)) @ w_down


def moe_swiglu(h, router_w, w_gate, w_up, w_down):
    n, d = h.shape
    nk = n * TOP_K
    num_blocks = -(-nk // MOE_BLOCK) + N_EXPERTS
    rows = num_blocks * MOE_BLOCK
    logits = (h @ router_w).astype(jnp.float32)
    top_logits, top_idx = lax.top_k(logits, TOP_K)
    gates = jax.nn.softmax(top_logits, axis=-1)
    flat_e = top_idx.reshape(nk)
    flat_tok = jnp.repeat(jnp.arange(n, dtype=jnp.int32), TOP_K)
    flat_g = gates.reshape(nk)
    order = jnp.argsort(flat_e)
    se, st, sg = flat_e[order], flat_tok[order], flat_g[order]
    counts = jnp.bincount(flat_e, length=N_EXPERTS)
    starts = jnp.cumsum(counts) - counts
    padded = (counts + MOE_BLOCK - 1) // MOE_BLOCK * MOE_BLOCK
    pad_ends = jnp.cumsum(padded)
    pad_starts = pad_ends - padded
    dest = pad_starts[se] + jnp.arange(nk) - starts[se]
    buf_tok = jnp.zeros((rows,), jnp.int32).at[dest].set(st)
    buf_gate = jnp.zeros((rows,), jnp.float32).at[dest].set(sg)
    blk_expert = jnp.minimum(
        jnp.searchsorted(pad_ends, jnp.arange(num_blocks) * MOE_BLOCK, side='right'), N_EXPERTS - 1)
    xb = h[buf_tok].reshape(num_blocks, MOE_BLOCK, d)

    def expert_block(args):
        xblk, e = args
        return swiglu(xblk, w_gate[e], w_up[e], w_down[e])

    yb = lax.map(expert_block, (xb, blk_expert)).reshape(rows, d)
    yb = yb.astype(jnp.float32) * buf_gate[:, None]
    out = jnp.zeros((n, d), jnp.float32).at[buf_tok].add(yb)
    return out.astype(h.dtype)


def setup_inputs(seed: int = 0) -> dict:
    key = jax.random.key(seed)
    ks = iter(jax.random.split(key, 64))
    f32 = jnp.float32
    n_dense = (DEPTH + 1) // 2
    n_moe = DEPTH // 2

    def nrm(shape, scale):
        return jax.random.normal(next(ks), shape, f32) * scale

    def gain(shape):
        return 1.0 + 0.02 * jax.random.normal(next(ks), shape, f32)

    def log_uniform(shape, lo, hi):
        return jax.random.uniform(next(ks), shape, f32, minval=math.log(lo), maxval=math.log(hi))

    x = nrm((BATCH, SEQ, D_MODEL), 1.0)
    mem = nrm((BATCH, MEM_LEN, D_MODEL), 1.0)
    norm_mix = gain((DEPTH, D_MODEL))
    w_in = nrm((DEPTH, D_MODEL, IN_COLS), D_MODEL ** -0.5)
    w_out = nrm((DEPTH, MIX_W, D_MODEL), MIX_W ** -0.5)
    hgrn_lb_logits = nrm((DEPTH, HA_W_K), 0.1)
    hgrn_norm = gain((DEPTH, HA_DV))
    gdn_conv = nrm((DEPTH, GB_CONV, CONV_CH), GB_CONV ** -0.5)
    gdn_a_log = jnp.log(jax.random.uniform(next(ks), (DEPTH, GB_HEADS), f32, minval=1.0, maxval=16.0))
    dt = jnp.exp(log_uniform((DEPTH, GB_HEADS), 1e-3, 1e-1))
    gdn_dt_bias = dt + jnp.log(-jnp.expm1(-dt))
    gdn_norm = gain((DEPTH, GB_DV))
    s5_a_re = -0.5 + nrm((DEPTH, S5_GROUPS, S5_STATE), 0.01)
    s5_a_im = (jnp.pi * jnp.arange(S5_STATE, dtype=f32))[None, None, :] + nrm((DEPTH, S5_GROUPS, S5_STATE), 0.01)
    s5_b_re = nrm((DEPTH, S5_GROUPS, S5_STATE, S5_CH), (2 * S5_CH) ** -0.5)
    s5_b_im = nrm((DEPTH, S5_GROUPS, S5_STATE, S5_CH), (2 * S5_CH) ** -0.5)
    s5_c_re = nrm((DEPTH, S5_GROUPS, S5_CH, S5_STATE), S5_STATE ** -0.5)
    s5_c_im = nrm((DEPTH, S5_GROUPS, S5_CH, S5_STATE), S5_STATE ** -0.5)
    s5_d = nrm((DEPTH, S5_GROUPS, S5_CH), 1.0)
    s5_log_step = log_uniform((DEPTH, S5_GROUPS), S5_DT_MIN, S5_DT_MAX)
    s5_w_glu = nrm((DEPTH, S5_W, S5_W), S5_W ** -0.5)
    s5_b_glu = nrm((DEPTH, S5_W), 0.01)
    norm_cross = gain((DEPTH, D_MODEL))
    norm_mem = gain((DEPTH, D_MODEL))
    xa_wq = nrm((DEPTH, D_MODEL, XA_W), D_MODEL ** -0.5)
    xa_wk = nrm((DEPTH, D_MODEL, XA_W), D_MODEL ** -0.5)
    xa_wv = nrm((DEPTH, D_MODEL, XA_W), D_MODEL ** -0.5)
    xa_wo = nrm((DEPTH, XA_W, D_MODEL), XA_W ** -0.5)
    norm_ffn = gain((DEPTH, D_MODEL))
    ffn_w_gate = nrm((n_dense, D_MODEL, D_FF), D_MODEL ** -0.5)
    ffn_w_up = nrm((n_dense, D_MODEL, D_FF), D_MODEL ** -0.5)
    ffn_w_down = nrm((n_dense, D_FF, D_MODEL), D_FF ** -0.5)
    moe_router = nrm((n_moe, D_MODEL, N_EXPERTS), D_MODEL ** -0.5)
    moe_w_gate = nrm((n_moe, N_EXPERTS, D_MODEL, D_FF), D_MODEL ** -0.5)
    moe_w_up = nrm((n_moe, N_EXPERTS, D_MODEL, D_FF), D_MODEL ** -0.5)
    moe_w_down = nrm((n_moe, N_EXPERTS, D_FF, D_MODEL), D_FF ** -0.5)
    norm_final = gain((D_MODEL,))
    return {
        "x": x, "mem": mem, "norm_mix": norm_mix, "w_in": w_in, "w_out": w_out,
        "hgrn_lb_logits": hgrn_lb_logits, "hgrn_norm": hgrn_norm,
        "gdn_conv": gdn_conv, "gdn_a_log": gdn_a_log, "gdn_dt_bias": gdn_dt_bias, "gdn_norm": gdn_norm,
        "s5_a_re": s5_a_re, "s5_a_im": s5_a_im, "s5_b_re": s5_b_re, "s5_b_im": s5_b_im,
        "s5_c_re": s5_c_re, "s5_c_im": s5_c_im, "s5_d": s5_d, "s5_log_step": s5_log_step,
        "s5_w_glu": s5_w_glu, "s5_b_glu": s5_b_glu,
        "norm_cross": norm_cross, "norm_mem": norm_mem,
        "xa_wq": xa_wq, "xa_wk": xa_wk, "xa_wv": xa_wv, "xa_wo": xa_wo,
        "norm_ffn": norm_ffn, "ffn_w_gate": ffn_w_gate, "ffn_w_up": ffn_w_up, "ffn_w_down": ffn_w_down,
        "moe_router": moe_router, "moe_w_gate": moe_w_gate, "moe_w_up": moe_w_up, "moe_w_down": moe_w_down,
        "norm_final": norm_final,
    }


def reference(x, mem, norm_mix, w_in, w_out, hgrn_lb_logits, hgrn_norm,
              gdn_conv, gdn_a_log, gdn_dt_bias, gdn_norm,
              s5_a_re, s5_a_im, s5_b_re, s5_b_im, s5_c_re, s5_c_im, s5_d, s5_log_step,
              s5_w_glu, s5_b_glu, norm_cross, norm_mem, xa_wq, xa_wk, xa_wv, xa_wo,
              norm_ffn, ffn_w_gate, ffn_w_up, ffn_w_down,
              moe_router, moe_w_gate, moe_w_up, moe_w_down, norm_final):
    bsz, s, d = x.shape
    lower_bounds = hgrn_lower_bounds(hgrn_lb_logits)
    h = x
    for layer in range(DEPTH):
        hn = rms_norm(h, norm_mix[layer])
        h = h + hybrid_mixer(hn, w_in[layer], w_out[layer], lower_bounds[layer], hgrn_norm[layer],
                             gdn_conv[layer], gdn_a_log[layer], gdn_dt_bias[layer], gdn_norm[layer],
                             s5_a_re[layer], s5_a_im[layer], s5_b_re[layer], s5_b_im[layer],
                             s5_c_re[layer], s5_c_im[layer], s5_d[layer], s5_log_step[layer],
                             s5_w_glu[layer], s5_b_glu[layer])
        hn = rms_norm(h, norm_cross[layer])
        mem_n = rms_norm(mem, norm_mem[layer])
        h = h + cross_attention(hn, mem_n, xa_wq[layer], xa_wk[layer], xa_wv[layer], xa_wo[layer])
        hn = rms_norm(h, norm_ffn[layer])
        i = layer // 2
        if layer % 2 == 0:
            h = h + swiglu(hn, ffn_w_gate[i], ffn_w_up[i], ffn_w_down[i])
        else:
            h = h + moe_swiglu(hn.reshape(bsz * s, d), moe_router[i], moe_w_gate[i], moe_w_up[i],
                               moe_w_down[i]).reshape(bsz, s, d)
    return rms_norm(h, norm_final)
```

```python
import functools
import math

import jax
import jax.numpy as jnp
from jax import lax
from jax.experimental import pallas as pl
from jax.experimental.pallas import tpu as pltpu

F32 = jnp.float32
BF16 = jnp.bfloat16

D_MODEL = 2048
N_HEADS = 6
HEAD_DIM = 128
CHUNK = 64
SUB = 16
CONV_K = 4
S5_GROUPS = 32
S5_CH = 16
S5_STATE = 64
S5_W = S5_GROUPS * S5_CH
S5_L = 16
S5_PAIRS = S5_GROUPS // 2
XA_HEADS = 4
XA_DH = 128
XA_W = XA_HEADS * XA_DH
D_FF = 5632
N_EXPERTS = 8
NORM_EPS = 1e-6
LB_FLOOR = 1e-30
LANES = 128
MIX_HEADS_W = N_HEADS * HEAD_DIM
IN_MAIN = 8 * MIX_HEADS_W
IN_PAD = IN_MAIN + S5_W + 512
GATE_BLK = (IN_MAIN + S5_W) // LANES
MOE_TM = 512
VMEM_LIMIT = 56 * 1024 * 1024


def _cparams(sem, vmem=VMEM_LIMIT):
    return pltpu.CompilerParams(dimension_semantics=sem, vmem_limit_bytes=vmem)


def _dot(a, b):
    return jnp.dot(a.astype(BF16), b.astype(BF16), preferred_element_type=F32)


def _dot_nt(a, b):
    return lax.dot_general(a.astype(BF16), b.astype(BF16), (((1,), (1,)), ((), ())),
                           preferred_element_type=F32)


def _dot_tn(a, b):
    return lax.dot_general(a.astype(BF16), b.astype(BF16), (((0,), (0,)), ((), ())),
                           preferred_element_type=F32)


def _split3(x):
    hi = x.astype(BF16)
    r1 = x - hi.astype(F32)
    mid = r1.astype(BF16)
    lo = (r1 - mid.astype(F32)).astype(BF16)
    return hi, mid, lo


def _dot_exact_lhs(m_bf16, x):
    hi, mid, lo = _split3(x)
    return (jnp.dot(m_bf16, hi, preferred_element_type=F32)
            + jnp.dot(m_bf16, mid, preferred_element_type=F32)
            + jnp.dot(m_bf16, lo, preferred_element_type=F32))


def _rms(x, w):
    ms = jnp.mean(x * x, axis=-1, keepdims=True)
    return x * lax.rsqrt(ms + NORM_EPS) * w


def _silu(x):
    return x * jax.nn.sigmoid(x)


def _norm_matmul_kernel(x_ref, nw_ref, w_ref, o_ref, xn_ref):
    @pl.when(pl.program_id(1) == 0)
    def _():
        xn_ref[...] = _rms(x_ref[...], nw_ref[...]).astype(BF16)

    o_ref[...] = jnp.dot(xn_ref[...], w_ref[...], preferred_element_type=F32).astype(o_ref.dtype)


def norm_matmul(x, nw, w, *, tm, tn, out_dtype=F32):
    m, k = x.shape
    n = w.shape[1]
    return pl.pallas_call(
        _norm_matmul_kernel,
        grid=(m // tm, n // tn),
        in_specs=[pl.BlockSpec((tm, k), lambda i, j: (i, 0)),
                  pl.BlockSpec((1, k), lambda i, j: (0, 0)),
                  pl.BlockSpec((k, tn), lambda i, j: (0, j))],
        out_specs=pl.BlockSpec((tm, tn), lambda i, j: (i, j)),
        out_shape=jax.ShapeDtypeStruct((m, n), out_dtype),
        scratch_shapes=[pltpu.VMEM((tm, k), BF16)],
        compiler_params=_cparams(("parallel", "arbitrary")),
        name="norm_matmul",
    )(x, nw.reshape(1, k), w)


def _chunk_masks():
    r = lax.broadcasted_iota(jnp.int32, (CHUNK, CHUNK), 0)
    c = lax.broadcasted_iota(jnp.int32, (CHUNK, CHUNK), 1)
    return r, c


def _hgrn_kernel(q_ref, f_ref, i_ref, g_ref, lb_ref, nw_ref, o_ref, st_ref, *, nchunk):
    @pl.when(pl.program_id(1) == 0)
    def _():
        st_ref[...] = jnp.zeros_like(st_ref)

    nsub = CHUNK // SUB
    lb = lb_ref[...]
    lbf = jnp.maximum(lb, LB_FLOOR)
    oml = 1.0 - lb
    nw = nw_ref[...]
    r, c = _chunk_masks()
    rb = lax.shift_right_logical(r, 4)
    cb = lax.shift_right_logical(c, 4)
    mats = [(c <= r)] + [(c < SUB * (j + 1)) for j in range(nsub)]
    pmat = jnp.concatenate([m.astype(BF16) for m in mats], axis=0)
    diag_mask = (rb == cb) & (c <= r)
    rcol = lax.broadcasted_iota(jnp.int32, (CHUNK, 1), 0)
    rbcol = lax.shift_right_logical(rcol, 4)

    def body(ci, carry):
        sl = pl.ds(pl.multiple_of(ci * CHUNK, CHUNK), CHUNK)
        fa = f_ref[sl, :]
        qa = q_ref[sl, :]
        v = i_ref[sl, :]
        ga = g_ref[sl, :]
        logf = jnp.log(lbf + oml * jax.nn.sigmoid(fa))
        k = oml * jax.nn.sigmoid(-fa)
        q = _silu(qa) * (HEAD_DIM ** -0.5)
        sums = _dot_exact_lhs(pmat, logf)
        cum = sums[0:CHUNK]
        ends = [sums[CHUNK * (j + 1):CHUNK * (j + 2)] for j in range(nsub)]
        last = ends[nsub - 1]
        base = jnp.zeros_like(cum)
        endv = ends[0]
        for j in range(1, nsub):
            base = jnp.where(rbcol >= j, ends[j - 1], base)
            endv = jnp.where(rbcol >= j, ends[j], endv)
        attn = jnp.where(diag_mask, _dot_nt(q * jnp.exp(cum - base), k * jnp.exp(base - cum)), 0.0)
        ko = k * jnp.exp(endv - cum)
        for j in range(nsub - 1):
            below = rbcol > j
            qo = jnp.where(below, q * jnp.exp(jnp.where(below, cum - ends[j], 0.0)), 0.0)
            kj = jnp.where(rbcol == j, ko, 0.0)
            attn = attn + _dot_nt(qo, kj)
        st = st_ref[...]
        o = _dot_nt(q * jnp.exp(cum), st) + _dot(attn, v)
        st_ref[...] = st * jnp.exp(last[0:1, :]) + _dot_tn(v, k * jnp.exp(last - cum))
        o_ref[sl, :] = (_rms(o, nw) * _silu(ga)).astype(o_ref.dtype)
        return carry

    lax.fori_loop(0, nchunk, body, 0)


def hgrn_mixer(proj, lb, nw, *, tb):
    t = proj.shape[0]
    nchunk = tb // CHUNK

    def col(off):
        return pl.BlockSpec((tb, HEAD_DIM), lambda h, i, off=off: (i, off + h))

    return pl.pallas_call(
        functools.partial(_hgrn_kernel, nchunk=nchunk),
        grid=(N_HEADS, t // tb),
        in_specs=[col(0), col(N_HEADS), col(2 * N_HEADS), col(3 * N_HEADS),
                  pl.BlockSpec((1, HEAD_DIM), lambda h, i: (0, h)),
                  pl.BlockSpec((1, HEAD_DIM), lambda h, i: (0, 0))],
        out_specs=pl.BlockSpec((tb, HEAD_DIM), lambda h, i: (i, h)),
        out_shape=jax.ShapeDtypeStruct((t, MIX_HEADS_W), BF16),
        scratch_shapes=[pltpu.VMEM((HEAD_DIM, HEAD_DIM), F32)],
        compiler_params=_cparams(("parallel", "arbitrary")),
        name="hgrn_mixer",
    )(proj, proj, proj, proj, lb.reshape(1, MIX_HEADS_W), nw.reshape(1, HEAD_DIM))


def _lane_pick(x, lane_ids, idx):
    return jnp.sum(jnp.where(lane_ids == idx, x, 0.0), axis=1, keepdims=True)


def _gdn_kernel(q_ref, k_ref, v_ref, z_ref, ab_ref, cq_ref, ck_ref, cv_ref, alog_ref, dtb_ref, nw_ref,
                o_ref, st_ref, xq_ref, xk_ref, xv_ref, *, nchunk, tb):
    head = pl.program_id(0)
    tail = 8

    @pl.when(pl.program_id(1) == 0)
    def _():
        st_ref[...] = jnp.zeros_like(st_ref)
        for xr in (xq_ref, xk_ref, xv_ref):
            xr[0:tail, :] = jnp.zeros((tail, HEAD_DIM), F32)

    for src, xr, cw in ((q_ref, xq_ref, cq_ref), (k_ref, xk_ref, ck_ref), (v_ref, xv_ref, cv_ref)):
        xr[tail:tail + tb, :] = src[...]
        acc = jnp.zeros((tb, HEAD_DIM), F32)
        for j in range(CONV_K):
            acc = acc + cw[j:j + 1, :] * xr[pl.ds(tail - (CONV_K - 1) + j, tb), :]
        new_tail = xr[tb:tb + tail, :]
        xr[tail:tail + tb, :] = _silu(acc)
        xr[0:tail, :] = new_tail

    nw = nw_ref[...]
    r, c = _chunk_masks()
    incl = c <= r
    strict = c < r
    eye = c == r
    tri = incl.astype(BF16)
    ones = jnp.ones((CHUNK, CHUNK), BF16)
    lane = lax.broadcasted_iota(jnp.int32, (CHUNK, LANES), 1)
    neg_a = -jnp.exp(alog_ref[...])
    dtb = dtb_ref[...]

    def body(ci, carry):
        off = pl.multiple_of(ci * CHUNK, CHUNK)
        sl = pl.ds(off, CHUNK)
        slx = pl.ds(off + tail, CHUNK)
        qc = xq_ref[slx, :]
        kc = xk_ref[slx, :]
        v = xv_ref[slx, :]
        q = qc * lax.rsqrt(jnp.sum(qc * qc, axis=-1, keepdims=True) + 1e-6) * (HEAD_DIM ** -0.5)
        k = kc * lax.rsqrt(jnp.sum(kc * kc, axis=-1, keepdims=True) + 1e-6)
        gates = ab_ref[sl, :]
        g_all = neg_a * jax.nn.softplus(gates + dtb)
        cum = _lane_pick(_dot_exact_lhs(tri, g_all), lane, head)
        beta = _lane_pick(jax.nn.sigmoid(gates), lane, head + N_HEADS)
        cum_b = jnp.broadcast_to(cum, (CHUNK, CHUNK))
        cum_row = _dot_exact_lhs(ones, jnp.where(eye, cum_b, 0.0))
        gam = jnp.where(incl, jnp.exp(jnp.where(incl, cum_b - cum_row, 0.0)), 0.0)
        last = cum[CHUNK - 1:CHUNK, :]
        ecum = jnp.exp(cum)
        a = jnp.where(strict, _dot_nt(k, k) * gam, 0.0) * beta
        eyef = eye.astype(F32)
        tinv = eyef - a
        pw = a
        for _ in range(int(math.log2(CHUNK)) - 1):
            pw = _dot(pw, pw)
            tinv = tinv + _dot(tinv, pw)
        rhs = jnp.concatenate([v * beta, k * (beta * ecum)], axis=1)
        sol = _dot(tinv, rhs)
        u = sol[:, :HEAD_DIM]
        w = sol[:, HEAD_DIM:]
        st = st_ref[...]
        v_new = u - _dot(w, st)
        qk = _dot_nt(q, k) * gam
        o = _dot(q * ecum, st) + _dot(qk, v_new)
        st_ref[...] = st * jnp.exp(last) + _dot_tn(k * jnp.exp(last - cum), v_new)
        o_ref[sl, :] = (_rms(o, nw) * _silu(z_ref[sl, :])).astype(o_ref.dtype)
        return carry

    lax.fori_loop(0, nchunk, body, 0)


def gdn_mixer(proj, conv_w, a_log, dt_bias, nw, *, tb):
    t = proj.shape[0]
    nchunk = tb // CHUNK
    base = 4 * N_HEADS

    def col(off):
        return pl.BlockSpec((tb, HEAD_DIM), lambda h, i, off=off: (i, off + h))

    def cw(off):
        return pl.BlockSpec((CONV_K, HEAD_DIM), lambda h, i, off=off: (0, off + h))

    def row_pad(vec):
        return jnp.zeros((1, LANES), F32).at[0, :N_HEADS].set(vec.astype(F32))

    dtb = jnp.zeros((1, LANES), F32).at[0, :N_HEADS].set(dt_bias.astype(F32))
    return pl.pallas_call(
        functools.partial(_gdn_kernel, nchunk=nchunk, tb=tb),
        grid=(N_HEADS, t // tb),
        in_specs=[col(base), col(base + N_HEADS), col(base + 2 * N_HEADS), col(base + 3 * N_HEADS),
                  pl.BlockSpec((tb, LANES), lambda h, i: (i, GATE_BLK)),
                  cw(0), cw(N_HEADS), cw(2 * N_HEADS),
                  pl.BlockSpec((1, LANES), lambda h, i: (0, 0)),
                  pl.BlockSpec((1, LANES), lambda h, i: (0, 0)),
                  pl.BlockSpec((1, HEAD_DIM), lambda h, i: (0, 0))],
        out_specs=pl.BlockSpec((tb, HEAD_DIM), lambda h, i: (i, h)),
        out_shape=jax.ShapeDtypeStruct((t, MIX_HEADS_W), BF16),
        scratch_shapes=[pltpu.VMEM((HEAD_DIM, HEAD_DIM), F32)]
        + [pltpu.VMEM((tb + 8, HEAD_DIM), F32) for _ in range(3)],
        compiler_params=_cparams(("parallel", "arbitrary")),
        name="gdn_mixer",
    )(proj, proj, proj, proj, proj, conv_w, conv_w, conv_w, row_pad(a_log), dtb, nw.reshape(1, HEAD_DIM))


_S5_PW = S5_L * 2 * S5_CH
_S5_SW = 2 * S5_STATE


def _dot3(a, b):
    ah = a.astype(BF16)
    al = (a - ah.astype(F32)).astype(BF16)
    bh = b.astype(BF16)
    bl = (b - bh.astype(F32)).astype(BF16)
    return (jnp.dot(ah, bh, preferred_element_type=F32) + jnp.dot(ah, bl, preferred_element_type=F32)
            + jnp.dot(al, bh, preferred_element_type=F32))


def _s5_tables_kernel(arc_ref, aic_ref, lsc_ref, arr_ref, air_ref, lsr_ref, cre_ref, cim_ref, bre_ref, bim_ref,
                      wcat_ref, ws2o_ref, alr_ref, ali_ref):
    are = jnp.minimum(arc_ref[0], -1e-4)
    aim = aic_ref[0]
    delta = jnp.exp(lsc_ref[0])
    lane = lax.broadcasted_iota(jnp.int32, (1, _S5_PW), 1)
    tau = lax.shift_right_logical(lane, 5).astype(F32)
    mag = jnp.exp(are * delta * tau)
    ang = aim * delta * tau
    lr = mag * jnp.cos(ang)
    li = mag * jnp.sin(ang)
    cre = cre_ref[0]
    cim = cim_ref[0]
    rr0 = jnp.concatenate([lr * cre - li * cim, -(lr * cim + li * cre)], axis=0)
    m1 = jnp.exp(are * delta)
    l1r = m1 * jnp.cos(aim * delta)
    l1i = m1 * jnp.sin(aim * delta)
    lr1 = lr * l1r - li * l1i
    li1 = lr * l1i + li * l1r
    ws2o_ref[0] = jnp.concatenate([lr1 * cre - li1 * cim, -(lr1 * cim + li1 * cre)], axis=0).astype(BF16)

    ar = jnp.minimum(arr_ref[0], -1e-4)
    ai = air_ref[0]
    dl = jnp.exp(lsr_ref[0])
    mb = jnp.exp(ar * dl)
    lbr = mb * jnp.cos(ai * dl)
    lbi = mb * jnp.sin(ai * dl)
    den = ar * ar + ai * ai
    xr = lbr - 1.0
    cr = (xr * ar + lbi * ai) / den
    ci = (lbi * ar - xr * ai) / den
    bre = bre_ref[0]
    bim = bim_ref[0]
    bbr = cr * bre - ci * bim
    bbi = cr * bim + ci * bre
    g = _dot3(jnp.concatenate([bbr, bbi], axis=1), rr0)
    rows_per = 2 * S5_CH
    for s in range(S5_L):
        if s == 0:
            blk = g
        else:
            blk = jnp.where(lane >= rows_per * s, pltpu.roll(g, rows_per * s, axis=1), 0.0)
        wcat_ref[0, rows_per * s:rows_per * (s + 1), 0:_S5_PW] = blk.astype(BF16)
    row = lax.broadcasted_iota(jnp.int32, (_S5_PW, 1), 0)
    mpow = (S5_L - 1 - lax.shift_right_logical(row, 5)).astype(F32)
    pm = jnp.exp(ar * dl * mpow)
    pr = pm * jnp.cos(ai * dl * mpow)
    pi = pm * jnp.sin(ai * dl * mpow)
    bbr_t = jnp.concatenate([bbr] * S5_L, axis=0)
    bbi_t = jnp.concatenate([bbi] * S5_L, axis=0)
    wcat_ref[0, :, _S5_PW:_S5_PW + _S5_SW] = (pr * bbr_t - pi * bbi_t).astype(BF16)
    wcat_ref[0, :, _S5_PW + _S5_SW:] = (pr * bbi_t + pi * bbr_t).astype(BF16)
    ml = jnp.exp(ar * dl * S5_L)
    alr_ref[0] = ml * jnp.cos(ai * dl * S5_L)
    ali_ref[0] = ml * jnp.sin(ai * dl * S5_L)


def s5_tables(a_re, a_im, b_re, b_im, c_re, c_im, log_step):
    p = S5_PAIRS
    ls = jnp.broadcast_to(log_step.astype(F32)[:, None], (S5_GROUPS, S5_STATE))
    same = (jnp.eye(2, dtype=F32) > 0).reshape(1, 2, 2, 1, 1)

    def ctile(cm):
        x = jnp.where(same, cm.astype(F32).reshape(p, 1, 2, S5_CH, S5_STATE), 0.0)
        x = x.transpose(0, 1, 4, 2, 3)[:, :, :, None]
        return jnp.broadcast_to(x, (p, 2, S5_STATE, S5_L, 2, S5_CH)).reshape(p, _S5_SW, _S5_PW)

    def btile(bm):
        x = jnp.where(same, bm.astype(F32).reshape(p, 2, 1, S5_STATE, S5_CH), 0.0)
        return x.transpose(0, 1, 4, 2, 3).reshape(p, 2 * S5_CH, _S5_SW)

    col = lambda v: v.astype(F32).reshape(p, _S5_SW, 1)
    rowv = lambda v: v.astype(F32).reshape(p, 1, _S5_SW)
    spec = lambda shape: pl.BlockSpec((1,) + shape, lambda i: (i, 0, 0))
    return pl.pallas_call(
        _s5_tables_kernel,
        grid=(p,),
        in_specs=[spec((_S5_SW, 1))] * 3 + [spec((1, _S5_SW))] * 3
        + [spec((_S5_SW, _S5_PW))] * 2 + [spec((2 * S5_CH, _S5_SW))] * 2,
        out_specs=[spec((_S5_PW, _S5_PW + 2 * _S5_SW)), spec((2 * _S5_SW, _S5_PW)),
                   spec((1, _S5_SW)), spec((1, _S5_SW))],
        out_shape=[jax.ShapeDtypeStruct((p, _S5_PW, _S5_PW + 2 * _S5_SW), BF16),
                   jax.ShapeDtypeStruct((p, 2 * _S5_SW, _S5_PW), BF16),
                   jax.ShapeDtypeStruct((p, 1, _S5_SW), F32),
                   jax.ShapeDtypeStruct((p, 1, _S5_SW), F32)],
        compiler_params=_cparams(("parallel",)),
        name="s5_tables",
    )(col(a_re), col(a_im), col(ls), rowv(a_re), rowv(a_im), rowv(ls),
      ctile(c_re), ctile(c_im), btile(b_re), btile(b_im))


def _s5_in_kernel(u_ref, wcat_ref, d_ref, yi_ref, vre_ref, vim_ref):
    u = u_ref[0]
    ycat = jnp.dot(u.astype(BF16), wcat_ref[0], preferred_element_type=F32)
    yi_ref[0] = ycat[:, :_S5_PW] + d_ref[0] * u
    vre_ref[...] = ycat[:, _S5_PW:_S5_PW + _S5_SW]
    vim_ref[...] = ycat[:, _S5_PW + _S5_SW:]


def _s5_scan_kernel(vre_ref, vim_ref, alr_ref, ali_ref, xre_ref, xim_ref, *, nsteps):
    ar = alr_ref[...]
    ai = ali_ref[...]

    def body(j, carry):
        xr, xi = carry
        xre_ref[j] = xr
        xim_ref[j] = xi
        return (ar * xr - ai * xi + vre_ref[j], ar * xi + ai * xr + vim_ref[j])

    zero = jnp.zeros(ar.shape, F32)
    lax.fori_loop(0, nsteps, body, (zero, zero))


def _gelu_tanh(x):
    return 0.5 * x * (1.0 + jnp.tanh(math.sqrt(2.0 / math.pi) * (x + 0.044715 * (x * x * x))))


def _s5_out_kernel(yi_ref, xre_ref, xim_ref, ws2o_ref, y_ref):
    w = ws2o_ref[0]
    y = (yi_ref[0] + jnp.dot(xre_ref[...].astype(BF16), w[:_S5_SW], preferred_element_type=F32)
         + jnp.dot(xim_ref[...].astype(BF16), w[_S5_SW:], preferred_element_type=F32))
    y_ref[0] = _gelu_tanh(y)


def _glu_kernel(y_ref, w_ref, b_ref, o_ref):
    y = y_ref[...]
    z = jnp.dot(y.astype(BF16), w_ref[...], preferred_element_type=F32) + b_ref[...]
    o_ref[...] = (y * jax.nn.sigmoid(z)).astype(o_ref.dtype)


def s5_mixer(u, a_re, a_im, b_re, b_im, c_re, c_im, d_skip, log_step, w_glu, b_glu):
    t = u.shape[0]
    nj = t // S5_L
    p = S5_PAIRS
    wcat, ws2o, alr, ali = s5_tables(a_re, a_im, b_re, b_im, c_re, c_im, log_step)
    up = u.reshape(nj, S5_L, p, 2 * S5_CH).transpose(2, 0, 1, 3).reshape(p, nj, _S5_PW)
    dt = jnp.tile(d_skip.astype(F32).reshape(p, 1, 2 * S5_CH), (1, 1, S5_L))
    pair3 = lambda shape: pl.BlockSpec((1,) + shape, lambda i: (i, 0, 0))
    lanes = lambda: pl.BlockSpec((nj, _S5_SW), lambda i: (0, i))
    yi, vre, vim = pl.pallas_call(
        _s5_in_kernel,
        grid=(p,),
        in_specs=[pair3((nj, _S5_PW)), pair3((_S5_PW, _S5_PW + 2 * _S5_SW)), pair3((1, _S5_PW))],
        out_specs=[pair3((nj, _S5_PW)), lanes(), lanes()],
        out_shape=[jax.ShapeDtypeStruct((p, nj, _S5_PW), F32),
                   jax.ShapeDtypeStruct((nj, p * _S5_SW), F32),
                   jax.ShapeDtypeStruct((nj, p * _S5_SW), F32)],
        compiler_params=_cparams(("parallel",)),
        name="s5_in",
    )(up, wcat, dt)
    sw = p * _S5_SW
    tile = (8, sw // 8)
    lb = tile[1] // 2
    sblk = lambda: pl.BlockSpec((nj, 8, lb), lambda i: (0, 0, i))
    ablk = lambda: pl.BlockSpec((8, lb), lambda i: (0, i))
    xre, xim = pl.pallas_call(
        functools.partial(_s5_scan_kernel, nsteps=nj),
        grid=(2,),
        in_specs=[sblk(), sblk(), ablk(), ablk()],
        out_specs=[sblk(), sblk()],
        out_shape=[jax.ShapeDtypeStruct((nj,) + tile, F32)] * 2,
        compiler_params=_cparams(("parallel",)),
        name="s5_scan",
    )(vre.reshape((nj,) + tile), vim.reshape((nj,) + tile), alr.reshape(tile), ali.reshape(tile))
    yp = pl.pallas_call(
        _s5_out_kernel,
        grid=(p,),
        in_specs=[pair3((nj, _S5_PW)), lanes(), lanes(), pair3((2 * _S5_SW, _S5_PW))],
        out_specs=pair3((nj, _S5_PW)),
        out_shape=jax.ShapeDtypeStruct((p, nj, _S5_PW), F32),
        compiler_params=_cparams(("parallel",)),
        name="s5_out",
    )(yi, xre.reshape(nj, sw), xim.reshape(nj, sw), ws2o)
    y = yp.reshape(p, nj, S5_L, 2 * S5_CH).transpose(1, 2, 0, 3).reshape(t, S5_W)
    tm = min(t, 1024)
    return pl.pallas_call(
        _glu_kernel,
        grid=(t // tm,),
        in_specs=[pl.BlockSpec((tm, S5_W), lambda i: (i, 0)),
                  pl.BlockSpec((S5_W, S5_W), lambda i: (0, 0)),
                  pl.BlockSpec((1, S5_W), lambda i: (0, 0))],
        out_specs=pl.BlockSpec((tm, S5_W), lambda i: (i, 0)),
        out_shape=jax.ShapeDtypeStruct((t, S5_W), BF16),
        compiler_params=_cparams(("parallel",)),
        name="s5_glu",
    )(y, w_glu.astype(BF16), b_glu.astype(F32).reshape(1, S5_W))


def _lower_bounds_kernel(x_ref, o_ref):
    x = x_ref[...]
    e = jnp.exp(x - jnp.max(x, axis=0, keepdims=True))
    p = e / jnp.sum(e, axis=0, keepdims=True)
    run = p[0:1]
    rows = [run - p[0:1]]
    for i in range(1, x.shape[0]):
        run = run + p[i:i + 1]
        rows.append(run - p[0:1])
    o_ref[...] = jnp.concatenate(rows, axis=0)


def lower_bounds(logits):
    return pl.pallas_call(
        _lower_bounds_kernel,
        out_shape=jax.ShapeDtypeStruct(logits.shape, F32),
        name="hgrn_lower_bounds",
    )(logits.astype(F32))


def _out_proj_kernel(h_ref, oa_ref, ob_ref, yc_ref, w1_ref, w2_ref, w3_ref, o_ref):
    o_ref[...] = (h_ref[...]
                  + jnp.dot(oa_ref[...], w1_ref[...], preferred_element_type=F32)
                  + jnp.dot(ob_ref[...], w2_ref[...], preferred_element_type=F32)
                  + jnp.dot(yc_ref[...], w3_ref[...], preferred_element_type=F32))


def out_proj(h, oa, ob, yc, w, *, tm, tn):
    m, n = h.shape
    hw = MIX_HEADS_W
    return pl.pallas_call(
        _out_proj_kernel,
        grid=(m // tm, n // tn),
        in_specs=[pl.BlockSpec((tm, tn), lambda i, j: (i, j)),
                  pl.BlockSpec((tm, hw), lambda i, j: (i, 0)),
                  pl.BlockSpec((tm, hw), lambda i, j: (i, 0)),
                  pl.BlockSpec((tm, S5_W), lambda i, j: (i, 0)),
                  pl.BlockSpec((hw, tn), lambda i, j: (0, j)),
                  pl.BlockSpec((hw, tn), lambda i, j: (1, j)),
                  pl.BlockSpec((S5_W, tn), lambda i, j: (2 * hw // S5_W, j))],
        out_specs=pl.BlockSpec((tm, tn), lambda i, j: (i, j)),
        out_shape=jax.ShapeDtypeStruct((m, n), F32),
        compiler_params=_cparams(("parallel", "arbitrary")),
        name="out_proj",
    )(h, oa, ob, yc, w, w, w)


def _xattn_kernel(h_ref, nw_ref, wq_ref, kv_ref, wo_ref, o_ref):
    h = h_ref[...]
    q = jnp.dot(_rms(h, nw_ref[...]).astype(BF16), wq_ref[...], preferred_element_type=F32)
    kv = kv_ref[...]
    outs = []
    for hd in range(XA_HEADS):
        lo, hi = hd * XA_DH, (hd + 1) * XA_DH
        s = _dot_nt(q[:, lo:hi], kv[:, lo:hi]) * (XA_DH ** -0.5)
        e = jnp.exp(s - jnp.max(s, axis=-1, keepdims=True))
        p = e / jnp.sum(e, axis=-1, keepdims=True)
        outs.append(_dot(p, kv[:, XA_W + lo:XA_W + hi]))
    o = jnp.concatenate(outs, axis=1)
    o_ref[...] = h + _dot(o, wo_ref[...])


def xattn(h, nw, wq, kv, wo, *, tm):
    m, d = h.shape
    full = lambda a: pl.BlockSpec(a.shape, lambda i: (0, 0))
    nw = nw.reshape(1, d)
    return pl.pallas_call(
        _xattn_kernel,
        grid=(m // tm,),
        in_specs=[pl.BlockSpec((tm, d), lambda i: (i, 0)), full(nw), full(wq), full(kv), full(wo)],
        out_specs=pl.BlockSpec((tm, d), lambda i: (i, 0)),
        out_shape=jax.ShapeDtypeStruct((m, d), F32),
        compiler_params=_cparams(("parallel",)),
        name="xattn",
    )(h, nw, wq, kv, wo)


def _ffn_kernel(te_ref, nv_ref, x_ref, nw_ref, wg_ref, wu_ref, wd_ref, o_ref, xn_ref, acc_ref, *, norm, residual, nf):
    i = pl.program_id(0)
    f = pl.program_id(1)

    @pl.when(i < nv_ref[0])
    def _():
        @pl.when(f == 0)
        def _():
            x = x_ref[...]
            xn_ref[...] = (_rms(x, nw_ref[...]) if norm else x).astype(BF16)
            acc_ref[...] = jnp.zeros_like(acc_ref)

        xn = xn_ref[...]
        g = jnp.dot(xn, wg_ref[0], preferred_element_type=F32)
        u = jnp.dot(xn, wu_ref[0], preferred_element_type=F32)
        acc_ref[...] += jnp.dot((_silu(g) * u).astype(BF16), wd_ref[0], preferred_element_type=F32)

        @pl.when(f == nf - 1)
        def _():
            o_ref[...] = (acc_ref[...] + x_ref[...]) if residual else acc_ref[...]

    @pl.when((i >= nv_ref[0]) & (f == nf - 1))
    def _():
        o_ref[...] = jnp.zeros_like(o_ref)


def ffn(x, nw, wg, wu, wd, tile_expert, nvalid, *, tm, tf, norm, residual):
    m, d = x.shape
    dff = wg.shape[2]
    nf = dff // tf

    def row(i, f, te, nv):
        return (jnp.minimum(i, nv[0] - 1), 0)

    def wcol(i, f, te, nv):
        return (te[jnp.minimum(i, nv[0] - 1)], 0, jnp.where(i < nv[0], f, nf - 1))

    def wrow(i, f, te, nv):
        return (te[jnp.minimum(i, nv[0] - 1)], jnp.where(i < nv[0], f, nf - 1), 0)

    return pl.pallas_call(
        functools.partial(_ffn_kernel, norm=norm, residual=residual, nf=nf),
        grid_spec=pltpu.PrefetchScalarGridSpec(
            num_scalar_prefetch=2,
            grid=(m // tm, nf),
            in_specs=[pl.BlockSpec((tm, d), row),
                      pl.BlockSpec((1, d), lambda i, f, te, nv: (0, 0)),
                      pl.BlockSpec((1, d, tf), wcol),
                      pl.BlockSpec((1, d, tf), wcol),
                      pl.BlockSpec((1, tf, d), wrow)],
            out_specs=pl.BlockSpec((tm, d), lambda i, f, te, nv: (i, 0)),
            scratch_shapes=[pltpu.VMEM((tm, d), BF16), pltpu.VMEM((tm, d), F32)]),
        out_shape=jax.ShapeDtypeStruct((m, d), F32),
        compiler_params=_cparams(("arbitrary", "arbitrary")),
        name="ffn",
    )(tile_expert, nvalid, x, nw.reshape(1, d), wg, wu, wd)


def _router_kernel(h_ref, nw_ref, rw_ref, hn_ref, info_ref, cnt_ref, carry_ref, tri_ref, *, tm):
    @pl.when(pl.program_id(0) == 0)
    def _():
        carry_ref[...] = jnp.zeros_like(carry_ref)
        r = lax.broadcasted_iota(jnp.int32, (tm, tm), 0)
        c = lax.broadcasted_iota(jnp.int32, (tm, tm), 1)
        tri_ref[...] = (c < r).astype(BF16)

    hn = _rms(h_ref[...], nw_ref[...])
    hn_ref[...] = hn
    logits = _dot3(hn, rw_ref[...])
    lane = lax.broadcasted_iota(jnp.int32, (tm, LANES), 1).astype(F32)
    neg = -jnp.inf
    lg = jnp.where(lane < N_EXPERTS, logits, neg)
    m1 = jnp.max(lg, axis=1, keepdims=True)
    i1 = jnp.min(jnp.where(lg == m1, lane, float(LANES)), axis=1, keepdims=True)
    lg2 = jnp.where(lane == i1, neg, lg)
    m2 = jnp.max(lg2, axis=1, keepdims=True)
    i2 = jnp.min(jnp.where(lg2 == m2, lane, float(LANES)), axis=1, keepdims=True)
    e2 = jnp.exp(m2 - m1)
    g1 = 1.0 / (1.0 + e2)
    g2 = e2 / (1.0 + e2)
    cnt = jnp.where((lane == i1) | (lane == i2), 1.0, 0.0)
    carry = carry_ref[...]
    before = jnp.dot(tri_ref[...], cnt.astype(BF16), preferred_element_type=F32) + carry
    r1 = jnp.sum(jnp.where(lane == i1, before, 0.0), axis=1, keepdims=True)
    r2 = jnp.sum(jnp.where(lane == i2, before, 0.0), axis=1, keepdims=True)
    carry = carry + jnp.sum(cnt, axis=0, keepdims=True)
    carry_ref[...] = carry
    cnt_ref[...] = jnp.broadcast_to(carry, cnt_ref.shape)
    info = jnp.zeros((tm, LANES), F32)
    for ln, val in enumerate((i1, i2, r1, r2, g1, g2)):
        info = jnp.where(lane == ln, val, info)
    info_ref[...] = info


def router(h, nw, rw, *, tm):
    m, d = h.shape
    rwp = jnp.zeros((d, LANES), F32).at[:, :N_EXPERTS].set(rw.astype(F32))
    return pl.pallas_call(
        functools.partial(_router_kernel, tm=tm),
        grid=(m // tm,),
        in_specs=[pl.BlockSpec((tm, d), lambda i: (i, 0)),
                  pl.BlockSpec((1, d), lambda i: (0, 0)),
                  pl.BlockSpec((d, LANES), lambda i: (0, 0))],
        out_specs=[pl.BlockSpec((tm, d), lambda i: (i, 0)),
                   pl.BlockSpec((tm, LANES), lambda i: (i, 0)),
                   pl.BlockSpec((8, LANES), lambda i: (0, 0))],
        out_shape=[jax.ShapeDtypeStruct((m, d), F32),
                   jax.ShapeDtypeStruct((m, LANES), F32),
                   jax.ShapeDtypeStruct((8, LANES), F32)],
        scratch_shapes=[pltpu.VMEM((1, LANES), F32), pltpu.VMEM((tm, tm), BF16)],
        compiler_params=_cparams(("arbitrary",)),
        name="moe_router",
    )(h, nw.reshape(1, d), rwp)


def _dispatch_kernel(dest_ref, hn_ref, init_ref, xb_ref, sem, *, tm):
    del init_ref
    base = pl.program_id(0) * tm

    def row_copy(t, d):
        return pltpu.make_async_copy(hn_ref.at[pl.ds(t, 1)], xb_ref.at[pl.ds(d, 1)], sem)

    def issue(t, c):
        row_copy(t, dest_ref[2 * (base + t)]).start()
        row_copy(t, dest_ref[2 * (base + t) + 1]).start()
        return c

    def drain(t, c):
        row_copy(0, 0).wait()
        row_copy(0, 0).wait()
        return c

    lax.fori_loop(0, tm, issue, 0)
    lax.fori_loop(0, tm, drain, 0)


def dispatch(hn, dest, rows, *, tm):
    m, d = hn.shape
    return pl.pallas_call(
        functools.partial(_dispatch_kernel, tm=tm),
        grid_spec=pltpu.PrefetchScalarGridSpec(
            num_scalar_prefetch=1,
            grid=(m // tm,),
            in_specs=[pl.BlockSpec((tm, d), lambda i, dst: (i, 0)),
                      pl.BlockSpec(memory_space=pl.ANY)],
            out_specs=pl.BlockSpec(memory_space=pl.ANY),
            scratch_shapes=[pltpu.SemaphoreType.DMA(())]),
        out_shape=jax.ShapeDtypeStruct((rows, d), F32),
        input_output_aliases={2: 0},
        compiler_params=_cparams(("arbitrary",)),
        name="moe_dispatch",
    )(dest, hn, jnp.zeros((rows, d), F32))


def _combine_kernel(dest_ref, h_ref, info_ref, yb_ref, o_ref, buf_ref, sem, *, tm):
    base = pl.program_id(0) * tm

    def row_copy(t, d, slot):
        return pltpu.make_async_copy(yb_ref.at[pl.ds(d, 1)], buf_ref.at[slot, pl.ds(t, 1)], sem)

    def issue(t, c):
        row_copy(t, dest_ref[2 * (base + t)], 0).start()
        row_copy(t, dest_ref[2 * (base + t) + 1], 1).start()
        return c

    def drain(t, c):
        row_copy(0, 0, 0).wait()
        row_copy(0, 0, 1).wait()
        return c

    lax.fori_loop(0, tm, issue, 0)
    lax.fori_loop(0, tm, drain, 0)
    info = info_ref[...]
    o_ref[...] = h_ref[...] + (info[:, 4:5] * buf_ref[0] + info[:, 5:6] * buf_ref[1])


def combine(h, info, yb, dest, *, tm):
    m, d = h.shape
    return pl.pallas_call(
        functools.partial(_combine_kernel, tm=tm),
        grid_spec=pltpu.PrefetchScalarGridSpec(
            num_scalar_prefetch=1,
            grid=(m // tm,),
            in_specs=[pl.BlockSpec((tm, d), lambda i, dst: (i, 0)),
                      pl.BlockSpec((tm, LANES), lambda i, dst: (i, 0)),
                      pl.BlockSpec(memory_space=pl.ANY)],
            out_specs=pl.BlockSpec((tm, d), lambda i, dst: (i, 0)),
            scratch_shapes=[pltpu.VMEM((2, tm, d), F32), pltpu.SemaphoreType.DMA(())]),
        out_shape=jax.ShapeDtypeStruct((m, d), F32),
        compiler_params=_cparams(("arbitrary",)),
        name="moe_combine",
    )(dest, h, info, yb)


def moe_layer(h, nw, rw, wg, wu, wd):
    m, d = h.shape
    tm = MOE_TM
    ntiles = m * 2 // tm + N_EXPERTS
    hn, info, cnt = router(h, nw, rw, tm=tm)
    counts = cnt[0, :N_EXPERTS].astype(jnp.int32)
    seg = (counts + tm - 1) // tm * tm
    seg_end = jnp.cumsum(seg)
    seg_start = seg_end - seg
    ids = info[:, 0:2].astype(jnp.int32)
    dest = (seg_start[ids] + info[:, 2:4].astype(jnp.int32)).reshape(2 * m)
    tile_expert = jnp.minimum(
        jnp.searchsorted(seg_end, jnp.arange(ntiles, dtype=jnp.int32) * tm, side='right'),
        N_EXPERTS - 1).astype(jnp.int32)
    nvalid = (seg_end[-1:] // tm).astype(jnp.int32)
    xb = dispatch(hn, dest, ntiles * tm, tm=256)
    yb = ffn(xb, nw, wg.astype(BF16), wu.astype(BF16), wd.astype(BF16), tile_expert, nvalid,
             tm=tm, tf=512, norm=False, residual=False)
    return combine(h, info, yb, dest, tm=256)


def _final_norm_kernel(x_ref, w_ref, o_ref):
    o_ref[...] = _rms(x_ref[...], w_ref[...])


def final_norm(h, w, *, tm):
    m, d = h.shape
    return pl.pallas_call(
        _final_norm_kernel,
        grid=(m // tm,),
        in_specs=[pl.BlockSpec((tm, d), lambda i: (i, 0)), pl.BlockSpec((1, d), lambda i: (0, 0))],
        out_specs=pl.BlockSpec((tm, d), lambda i: (i, 0)),
        out_shape=jax.ShapeDtypeStruct((m, d), F32),
        compiler_params=_cparams(("parallel",)),
        name="final_norm",
    )(h, w.reshape(1, d))


def _pack_w_in(wi):
    gates = wi[:, IN_MAIN:IN_MAIN + 2 * N_HEADS]
    pad = jnp.zeros((wi.shape[0], IN_PAD - wi.shape[1]), wi.dtype)
    return jnp.concatenate([wi[:, :IN_MAIN], wi[:, IN_MAIN + 2 * N_HEADS:], gates, pad], axis=1).astype(BF16)


def kernel(x, mem, norm_mix, w_in, w_out, hgrn_lb_logits, hgrn_norm, gdn_conv, gdn_a_log, gdn_dt_bias, gdn_norm, s5_a_re, s5_a_im, s5_b_re, s5_b_im, s5_c_re, s5_c_im, s5_d, s5_log_step, s5_w_glu, s5_b_glu, norm_cross, norm_mem, xa_wq, xa_wk, xa_wv, xa_wo, norm_ffn, ffn_w_gate, ffn_w_up, ffn_w_down, moe_router, moe_w_gate, moe_w_up, moe_w_down, norm_final):
    bsz, t, d = x.shape
    assert bsz == 1
    depth = w_in.shape[0]
    h = x[0].astype(F32)
    memx = mem[0].astype(F32)
    lbs = lower_bounds(hgrn_lb_logits)
    tb = min(t, 512)
    tmr = min(t, 1024)
    dense_tiles = t // MOE_TM
    for layer in range(depth):
        proj = norm_matmul(h, norm_mix[layer], _pack_w_in(w_in[layer]), tm=tmr, tn=512)
        oa = hgrn_mixer(proj, lbs[layer], hgrn_norm[layer], tb=tb)
        ob = gdn_mixer(proj, gdn_conv[layer].astype(F32), gdn_a_log[layer], gdn_dt_bias[layer], gdn_norm[layer], tb=tb)
        yc = s5_mixer(proj[:, IN_MAIN:IN_MAIN + S5_W], s5_a_re[layer], s5_a_im[layer], s5_b_re[layer],
                      s5_b_im[layer], s5_c_re[layer], s5_c_im[layer], s5_d[layer], s5_log_step[layer],
                      s5_w_glu[layer], s5_b_glu[layer])
        h = out_proj(h, oa, ob, yc, w_out[layer].astype(BF16), tm=tmr, tn=512)
        wkv = jnp.concatenate([xa_wk[layer], xa_wv[layer]], axis=1).astype(BF16)
        kv = norm_matmul(memx, norm_mem[layer], wkv, tm=memx.shape[0], tn=512, out_dtype=BF16)
        h = xattn(h, norm_cross[layer], xa_wq[layer].astype(BF16), kv, xa_wo[layer].astype(BF16), tm=MOE_TM)
        i = layer // 2
        if layer % 2 == 0:
            h = ffn(h, norm_ffn[layer], ffn_w_gate[i:i + 1].astype(BF16), ffn_w_up[i:i + 1].astype(BF16),
                    ffn_w_down[i:i + 1].astype(BF16), jnp.zeros((dense_tiles,), jnp.int32),
                    jnp.full((1,), dense_tiles, jnp.int32), tm=MOE_TM, tf=512, norm=True, residual=True)
        else:
            h = moe_layer(h, norm_ffn[layer], moe_router[i], moe_w_gate[i], moe_w_up[i], moe_w_down[i])
    return final_norm(h, norm_final, tm=tmr)[None].astype(x.dtype)
```

```python
import functools
import math

import jax
import jax.numpy as jnp
from jax import lax
from jax.experimental import pallas as pl
from jax.experimental.pallas import tpu as pltpu

F32 = jnp.float32
BF16 = jnp.bfloat16

D_MODEL = 2048
N_HEADS = 6
HEAD_DIM = 128
CHUNK = 64
SUB = 16
CONV_K = 4
S5_GROUPS = 32
S5_CH = 16
S5_STATE = 64
S5_W = S5_GROUPS * S5_CH
S5_L = 16
S5_PAIRS = S5_GROUPS // 2
XA_HEADS = 4
XA_DH = 128
XA_W = XA_HEADS * XA_DH
D_FF = 5632
N_EXPERTS = 8
NORM_EPS = 1e-6
LB_FLOOR = 1e-30
LANES = 128
MIX_HEADS_W = N_HEADS * HEAD_DIM
IN_MAIN = 8 * MIX_HEADS_W
IN_TAIL = S5_W + LANES
HEADS_PER_BODY = 6
MOE_TM = 512
FFN_TF = 512
VMEM_LIMIT = 56 * 1024 * 1024


def _cparams(sem, vmem=VMEM_LIMIT):
    return pltpu.CompilerParams(dimension_semantics=sem, vmem_limit_bytes=vmem)


def _dot(a, b):
    return jnp.dot(a.astype(BF16), b.astype(BF16), preferred_element_type=F32)


def _dot_nt(a, b):
    return lax.dot_general(a.astype(BF16), b.astype(BF16), (((1,), (1,)), ((), ())),
                           preferred_element_type=F32)


def _dot_tn(a, b):
    return lax.dot_general(a.astype(BF16), b.astype(BF16), (((0,), (0,)), ((), ())),
                           preferred_element_type=F32)


def _split3(x):
    hi = x.astype(BF16)
    r1 = x - hi.astype(F32)
    mid = r1.astype(BF16)
    lo = (r1 - mid.astype(F32)).astype(BF16)
    return hi, mid, lo


def _dot_exact_lhs(m_bf16, x):
    hi, mid, lo = _split3(x)
    return (jnp.dot(m_bf16, hi, preferred_element_type=F32)
            + jnp.dot(m_bf16, mid, preferred_element_type=F32)
            + jnp.dot(m_bf16, lo, preferred_element_type=F32))


def _rms(x, w):
    ms = jnp.mean(x * x, axis=-1, keepdims=True)
    return x * lax.rsqrt(ms + NORM_EPS) * w


def _silu(x):
    return x * jax.nn.sigmoid(x)


def _norm_matmul_kernel(x_ref, nw_ref, w_ref, o_ref, xn_ref):
    @pl.when(pl.program_id(1) == 0)
    def _():
        xn_ref[...] = _rms(x_ref[...], nw_ref[...]).astype(BF16)

    o_ref[...] = jnp.dot(xn_ref[...], w_ref[0].astype(BF16), preferred_element_type=F32).astype(o_ref.dtype)


def norm_matmul(x, nw, w, layer, n, *, tm, tn, out_dtype=F32):
    m, k = x.shape
    return pl.pallas_call(
        _norm_matmul_kernel,
        grid=(m // tm, n // tn),
        in_specs=[pl.BlockSpec((tm, k), lambda i, j: (i, 0)),
                  pl.BlockSpec((1, k), lambda i, j: (0, 0)),
                  pl.BlockSpec((1, k, tn), lambda i, j: (layer, 0, j))],
        out_specs=pl.BlockSpec((tm, tn), lambda i, j: (i, j)),
        out_shape=jax.ShapeDtypeStruct((m, n), out_dtype),
        scratch_shapes=[pltpu.VMEM((tm, k), BF16)],
        compiler_params=_cparams(("parallel", "arbitrary")),
        name="norm_matmul",
    )(x, nw.reshape(1, k), w)


def _chunk_masks():
    r = lax.broadcasted_iota(jnp.int32, (CHUNK, CHUNK), 0)
    c = lax.broadcasted_iota(jnp.int32, (CHUNK, CHUNK), 1)
    return r, c


def _hgrn_kernel(q_ref, f_ref, i_ref, g_ref, lb_ref, nw_ref, o_ref, st_ref, *, nchunk, hpb):
    @pl.when(pl.program_id(1) == 0)
    def _():
        st_ref[...] = jnp.zeros_like(st_ref)

    nsub = CHUNK // SUB
    nw = nw_ref[...]
    r, c = _chunk_masks()
    rb = lax.shift_right_logical(r, 4)
    cb = lax.shift_right_logical(c, 4)
    mats = [(c <= r)] + [(c < SUB * (j + 1)) for j in range(nsub)]
    pmat = jnp.concatenate([m.astype(BF16) for m in mats], axis=0)
    diag_mask = (rb == cb) & (c <= r)
    rcol = lax.broadcasted_iota(jnp.int32, (CHUNK, 1), 0)
    rbcol = lax.shift_right_logical(rcol, 4)

    def head_chunk(sl, hd):
        hs = slice(hd * HEAD_DIM, (hd + 1) * HEAD_DIM)
        lb = lb_ref[:, hs]
        lbf = jnp.maximum(lb, LB_FLOOR)
        oml = 1.0 - lb
        fa = f_ref[sl, hs]
        qa = q_ref[sl, hs]
        v = i_ref[sl, hs]
        ga = g_ref[sl, hs]
        logf = jnp.log(lbf + oml * jax.nn.sigmoid(fa))
        k = oml * jax.nn.sigmoid(-fa)
        q = _silu(qa) * (HEAD_DIM ** -0.5)
        sums = _dot_exact_lhs(pmat, logf)
        cum = sums[0:CHUNK]
        ends = [sums[CHUNK * (j + 1):CHUNK * (j + 2)] for j in range(nsub)]
        last = ends[nsub - 1]
        base = jnp.zeros_like(cum)
        endv = ends[0]
        for j in range(1, nsub):
            base = jnp.where(rbcol >= j, ends[j - 1], base)
            endv = jnp.where(rbcol >= j, ends[j], endv)
        attn = jnp.where(diag_mask, _dot_nt(q * jnp.exp(cum - base), k * jnp.exp(base - cum)), 0.0)
        ko = k * jnp.exp(endv - cum)
        for j in range(nsub - 1):
            below = rbcol > j
            qo = jnp.where(below, q * jnp.exp(jnp.where(below, cum - ends[j], 0.0)), 0.0)
            kj = jnp.where(rbcol == j, ko, 0.0)
            attn = attn + _dot_nt(qo, kj)
        st = st_ref[hd]
        o = _dot_nt(q * jnp.exp(cum), st) + _dot(attn, v)
        st_ref[hd] = st * jnp.exp(last[0:1, :]) + _dot_tn(v, k * jnp.exp(last - cum))
        o_ref[sl, hs] = (_rms(o, nw) * _silu(ga)).astype(o_ref.dtype)

    def body(ci, carry):
        sl = pl.ds(pl.multiple_of(ci * CHUNK, CHUNK), CHUNK)
        for hd in range(hpb):
            head_chunk(sl, hd)
        return carry

    lax.fori_loop(0, nchunk, body, 0)


def hgrn_mixer(proj, lb, nw, *, tb, hpb):
    t = proj.shape[0]
    nchunk = tb // CHUNK
    ngrp = N_HEADS // hpb
    w = hpb * HEAD_DIM

    def col(part):
        return pl.BlockSpec((tb, w), lambda g, i, part=part: (i, part * ngrp + g))

    return pl.pallas_call(
        functools.partial(_hgrn_kernel, nchunk=nchunk, hpb=hpb),
        grid=(ngrp, t // tb),
        in_specs=[col(0), col(1), col(2), col(3),
                  pl.BlockSpec((1, w), lambda g, i: (0, g)),
                  pl.BlockSpec((1, HEAD_DIM), lambda g, i: (0, 0))],
        out_specs=pl.BlockSpec((tb, w), lambda g, i: (i, g)),
        out_shape=jax.ShapeDtypeStruct((t, MIX_HEADS_W), BF16),
        scratch_shapes=[pltpu.VMEM((hpb, HEAD_DIM, HEAD_DIM), F32)],
        compiler_params=_cparams(("parallel", "arbitrary")),
        name="hgrn_mixer",
    )(proj, proj, proj, proj, lb.reshape(1, MIX_HEADS_W), nw.reshape(1, HEAD_DIM))


def _lane_pick(x, lane_ids, idx):
    return jnp.sum(jnp.where(lane_ids == idx, x, 0.0), axis=1, keepdims=True)


def _gdn_kernel(q_ref, k_ref, v_ref, z_ref, ab_ref, cq_ref, ck_ref, cv_ref, alog_ref, dtb_ref, nw_ref,
                o_ref, st_ref, xq_ref, xk_ref, xv_ref, *, nchunk, tb, hpb):
    head0 = pl.program_id(0) * hpb
    tail = 8
    width = hpb * HEAD_DIM

    @pl.when(pl.program_id(1) == 0)
    def _():
        st_ref[...] = jnp.zeros_like(st_ref)
        for xr in (xq_ref, xk_ref, xv_ref):
            xr[0:tail, :] = jnp.zeros((tail, width), F32)

    for src, xr, cw in ((q_ref, xq_ref, cq_ref), (k_ref, xk_ref, ck_ref), (v_ref, xv_ref, cv_ref)):
        xr[tail:tail + tb, :] = src[...]
        acc = jnp.zeros((tb, width), F32)
        for j in range(CONV_K):
            acc = acc + cw[j:j + 1, :] * xr[pl.ds(tail - (CONV_K - 1) + j, tb), :]
        new_tail = xr[tb:tb + tail, :]
        xr[tail:tail + tb, :] = _silu(acc)
        xr[0:tail, :] = new_tail

    nw = nw_ref[...]
    r, c = _chunk_masks()
    incl = c <= r
    strict = c < r
    eye = c == r
    tri = incl.astype(BF16)
    ones = jnp.ones((CHUNK, CHUNK), BF16)
    lane = lax.broadcasted_iota(jnp.int32, (CHUNK, LANES), 1)
    neg_a = -jnp.exp(alog_ref[...])
    dtb = dtb_ref[...]

    def head_chunk(sl, slx, hd, cum_all, beta_all):
        hs = slice(hd * HEAD_DIM, (hd + 1) * HEAD_DIM)
        qc = xq_ref[slx, hs]
        kc = xk_ref[slx, hs]
        v = xv_ref[slx, hs]
        q = qc * lax.rsqrt(jnp.sum(qc * qc, axis=-1, keepdims=True) + 1e-6) * (HEAD_DIM ** -0.5)
        k = kc * lax.rsqrt(jnp.sum(kc * kc, axis=-1, keepdims=True) + 1e-6)
        cum = _lane_pick(cum_all, lane, head0 + hd)
        beta = _lane_pick(beta_all, lane, head0 + hd + N_HEADS)
        cum_b = jnp.broadcast_to(cum, (CHUNK, CHUNK))
        cum_row = _dot_exact_lhs(ones, jnp.where(eye, cum_b, 0.0))
        gam = jnp.where(incl, jnp.exp(jnp.where(incl, cum_b - cum_row, 0.0)), 0.0)
        last = cum[CHUNK - 1:CHUNK, :]
        ecum = jnp.exp(cum)
        a = jnp.where(strict, _dot_nt(k, k) * gam, 0.0) * beta
        eyef = eye.astype(F32)
        tinv = eyef - a
        pw = a
        for _ in range(int(math.log2(CHUNK)) - 1):
            pw = _dot(pw, pw)
            tinv = tinv + _dot(tinv, pw)
        rhs = jnp.concatenate([v * beta, k * (beta * ecum)], axis=1)
        sol = _dot(tinv, rhs)
        u = sol[:, :HEAD_DIM]
        w = sol[:, HEAD_DIM:]
        st = st_ref[hd]
        v_new = u - _dot(w, st)
        qk = _dot_nt(q, k) * gam
        o = _dot(q * ecum, st) + _dot(qk, v_new)
        st_ref[hd] = st * jnp.exp(last) + _dot_tn(k * jnp.exp(last - cum), v_new)
        o_ref[sl, hs] = (_rms(o, nw) * _silu(z_ref[sl, hs])).astype(o_ref.dtype)

    def body(ci, carry):
        off = pl.multiple_of(ci * CHUNK, CHUNK)
        sl = pl.ds(off, CHUNK)
        slx = pl.ds(off + tail, CHUNK)
        gates = ab_ref[sl, :]
        cum_all = _dot_exact_lhs(tri, neg_a * jax.nn.softplus(gates + dtb))
        beta_all = jax.nn.sigmoid(gates)
        for hd in range(hpb):
            head_chunk(sl, slx, hd, cum_all, beta_all)
        return carry

    lax.fori_loop(0, nchunk, body, 0)


def gdn_mixer(proj, gates, gate_blk, conv_w, a_log, dt_bias, nw, *, tb, hpb):
    t = proj.shape[0]
    nchunk = tb // CHUNK
    ngrp = N_HEADS // hpb
    w = hpb * HEAD_DIM

    def col(part):
        return pl.BlockSpec((tb, w), lambda g, i, part=part: (i, part * ngrp + g))

    def cw(part):
        return pl.BlockSpec((CONV_K, w), lambda g, i, part=part: (0, part * ngrp + g))

    def row_pad(vec):
        return jnp.zeros((1, LANES), F32).at[0, :N_HEADS].set(vec.astype(F32))

    return pl.pallas_call(
        functools.partial(_gdn_kernel, nchunk=nchunk, tb=tb, hpb=hpb),
        grid=(ngrp, t // tb),
        in_specs=[col(4), col(5), col(6), col(7),
                  pl.BlockSpec((tb, LANES), lambda g, i: (i, gate_blk)),
                  cw(0), cw(1), cw(2),
                  pl.BlockSpec((1, LANES), lambda g, i: (0, 0)),
                  pl.BlockSpec((1, LANES), lambda g, i: (0, 0)),
                  pl.BlockSpec((1, HEAD_DIM), lambda g, i: (0, 0))],
        out_specs=pl.BlockSpec((tb, w), lambda g, i: (i, g)),
        out_shape=jax.ShapeDtypeStruct((t, MIX_HEADS_W), BF16),
        scratch_shapes=[pltpu.VMEM((hpb, HEAD_DIM, HEAD_DIM), F32)]
        + [pltpu.VMEM((tb + 8, w), F32) for _ in range(3)],
        compiler_params=_cparams(("parallel", "arbitrary")),
        name="gdn_mixer",
    )(proj, proj, proj, proj, gates, conv_w, conv_w, conv_w, row_pad(a_log), row_pad(dt_bias),
      nw.reshape(1, HEAD_DIM))


_S5_PW = S5_L * 2 * S5_CH
_S5_SW = 2 * S5_STATE


def _dot3(a, b):
    ah = a.astype(BF16)
    al = (a - ah.astype(F32)).astype(BF16)
    bh = b.astype(BF16)
    bl = (b - bh.astype(F32)).astype(BF16)
    return (jnp.dot(ah, bh, preferred_element_type=F32) + jnp.dot(ah, bl, preferred_element_type=F32)
            + jnp.dot(al, bh, preferred_element_type=F32))


def _s5_tables_kernel(arc_ref, aic_ref, lsc_ref, arr_ref, air_ref, lsr_ref, cre_ref, cim_ref, bre_ref, bim_ref,
                      wcat_ref, ws2o_ref, alr_ref, ali_ref):
    are = jnp.minimum(arc_ref[0], -1e-4)
    aim = aic_ref[0]
    delta = jnp.exp(lsc_ref[0])
    lane = lax.broadcasted_iota(jnp.int32, (1, _S5_PW), 1)
    tau = lax.shift_right_logical(lane, 5).astype(F32)
    mag = jnp.exp(are * delta * tau)
    ang = aim * delta * tau
    lr = mag * jnp.cos(ang)
    li = mag * jnp.sin(ang)
    cre = cre_ref[0]
    cim = cim_ref[0]
    rr0 = jnp.concatenate([lr * cre - li * cim, -(lr * cim + li * cre)], axis=0)
    m1 = jnp.exp(are * delta)
    l1r = m1 * jnp.cos(aim * delta)
    l1i = m1 * jnp.sin(aim * delta)
    lr1 = lr * l1r - li * l1i
    li1 = lr * l1i + li * l1r
    ws2o_ref[0] = jnp.concatenate([lr1 * cre - li1 * cim, -(lr1 * cim + li1 * cre)], axis=0).astype(BF16)

    ar = jnp.minimum(arr_ref[0], -1e-4)
    ai = air_ref[0]
    dl = jnp.exp(lsr_ref[0])
    mb = jnp.exp(ar * dl)
    lbr = mb * jnp.cos(ai * dl)
    lbi = mb * jnp.sin(ai * dl)
    den = ar * ar + ai * ai
    xr = lbr - 1.0
    cr = (xr * ar + lbi * ai) / den
    ci = (lbi * ar - xr * ai) / den
    bre = bre_ref[0]
    bim = bim_ref[0]
    bbr = cr * bre - ci * bim
    bbi = cr * bim + ci * bre
    g = _dot3(jnp.concatenate([bbr, bbi], axis=1), rr0)
    rows_per = 2 * S5_CH
    for s in range(S5_L):
        if s == 0:
            blk = g
        else:
            blk = jnp.where(lane >= rows_per * s, pltpu.roll(g, rows_per * s, axis=1), 0.0)
        wcat_ref[0, rows_per * s:rows_per * (s + 1), 0:_S5_PW] = blk.astype(BF16)
    row = lax.broadcasted_iota(jnp.int32, (_S5_PW, 1), 0)
    mpow = (S5_L - 1 - lax.shift_right_logical(row, 5)).astype(F32)
    pm = jnp.exp(ar * dl * mpow)
    pr = pm * jnp.cos(ai * dl * mpow)
    pi = pm * jnp.sin(ai * dl * mpow)
    bbr_t = jnp.concatenate([bbr] * S5_L, axis=0)
    bbi_t = jnp.concatenate([bbi] * S5_L, axis=0)
    wcat_ref[0, :, _S5_PW:_S5_PW + _S5_SW] = (pr * bbr_t - pi * bbi_t).astype(BF16)
    wcat_ref[0, :, _S5_PW + _S5_SW:] = (pr * bbi_t + pi * bbr_t).astype(BF16)
    ml = jnp.exp(ar * dl * S5_L)
    alr_ref[0] = ml * jnp.cos(ai * dl * S5_L)
    ali_ref[0] = ml * jnp.sin(ai * dl * S5_L)


def s5_tables(a_re, a_im, b_re, b_im, c_re, c_im, log_step):
    p = S5_PAIRS
    ls = jnp.broadcast_to(log_step.astype(F32)[:, None], (S5_GROUPS, S5_STATE))
    same = (jnp.eye(2, dtype=F32) > 0).reshape(1, 2, 2, 1, 1)

    def ctile(cm):
        x = jnp.where(same, cm.astype(F32).reshape(p, 1, 2, S5_CH, S5_STATE), 0.0)
        x = x.transpose(0, 1, 4, 2, 3)[:, :, :, None]
        return jnp.broadcast_to(x, (p, 2, S5_STATE, S5_L, 2, S5_CH)).reshape(p, _S5_SW, _S5_PW)

    def btile(bm):
        x = jnp.where(same, bm.astype(F32).reshape(p, 2, 1, S5_STATE, S5_CH), 0.0)
        return x.transpose(0, 1, 4, 2, 3).reshape(p, 2 * S5_CH, _S5_SW)

    col = lambda v: v.astype(F32).reshape(p, _S5_SW, 1)
    rowv = lambda v: v.astype(F32).reshape(p, 1, _S5_SW)
    spec = lambda shape: pl.BlockSpec((1,) + shape, lambda i: (i, 0, 0))
    return pl.pallas_call(
        _s5_tables_kernel,
        grid=(p,),
        in_specs=[spec((_S5_SW, 1))] * 3 + [spec((1, _S5_SW))] * 3
        + [spec((_S5_SW, _S5_PW))] * 2 + [spec((2 * S5_CH, _S5_SW))] * 2,
        out_specs=[spec((_S5_PW, _S5_PW + 2 * _S5_SW)), spec((2 * _S5_SW, _S5_PW)),
                   spec((1, _S5_SW)), spec((1, _S5_SW))],
        out_shape=[jax.ShapeDtypeStruct((p, _S5_PW, _S5_PW + 2 * _S5_SW), BF16),
                   jax.ShapeDtypeStruct((p, 2 * _S5_SW, _S5_PW), BF16),
                   jax.ShapeDtypeStruct((p, 1, _S5_SW), F32),
                   jax.ShapeDtypeStruct((p, 1, _S5_SW), F32)],
        compiler_params=_cparams(("parallel",)),
        name="s5_tables",
    )(col(a_re), col(a_im), col(ls), rowv(a_re), rowv(a_im), rowv(ls),
      ctile(c_re), ctile(c_im), btile(b_re), btile(b_im))


def _s5_in_kernel(u_ref, wcat_ref, d_ref, yi_ref, vre_ref, vim_ref):
    u = u_ref[0]
    ycat = jnp.dot(u.astype(BF16), wcat_ref[0], preferred_element_type=F32)
    yi_ref[0] = ycat[:, :_S5_PW] + d_ref[0] * u
    vre_ref[...] = ycat[:, _S5_PW:_S5_PW + _S5_SW]
    vim_ref[...] = ycat[:, _S5_PW + _S5_SW:]


def _s5_scan_kernel(vre_ref, vim_ref, alr_ref, ali_ref, xre_ref, xim_ref, *, nsteps):
    ar = alr_ref[...]
    ai = ali_ref[...]

    def body(j, carry):
        xr, xi = carry
        xre_ref[j] = xr
        xim_ref[j] = xi
        return (ar * xr - ai * xi + vre_ref[j], ar * xi + ai * xr + vim_ref[j])

    zero = jnp.zeros(ar.shape, F32)
    lax.fori_loop(0, nsteps, body, (zero, zero))


def _gelu_tanh(x):
    return 0.5 * x * (1.0 + jnp.tanh(math.sqrt(2.0 / math.pi) * (x + 0.044715 * (x * x * x))))


def _s5_out_kernel(yi_ref, xre_ref, xim_ref, ws2o_ref, y_ref):
    w = ws2o_ref[0]
    y = (yi_ref[0] + jnp.dot(xre_ref[...].astype(BF16), w[:_S5_SW], preferred_element_type=F32)
         + jnp.dot(xim_ref[...].astype(BF16), w[_S5_SW:], preferred_element_type=F32))
    y_ref[0] = _gelu_tanh(y)


def _glu_kernel(y_ref, w_ref, b_ref, o_ref):
    y = y_ref[...]
    z = jnp.dot(y.astype(BF16), w_ref[...], preferred_element_type=F32) + b_ref[...]
    o_ref[...] = (y * jax.nn.sigmoid(z)).astype(o_ref.dtype)


def s5_mixer(u, a_re, a_im, b_re, b_im, c_re, c_im, d_skip, log_step, w_glu, b_glu):
    t = u.shape[0]
    nj = t // S5_L
    p = S5_PAIRS
    wcat, ws2o, alr, ali = s5_tables(a_re, a_im, b_re, b_im, c_re, c_im, log_step)
    up = u.reshape(nj, S5_L, p, 2 * S5_CH).transpose(2, 0, 1, 3).reshape(p, nj, _S5_PW)
    dt = jnp.tile(d_skip.astype(F32).reshape(p, 1, 2 * S5_CH), (1, 1, S5_L))
    pair3 = lambda shape: pl.BlockSpec((1,) + shape, lambda i: (i, 0, 0))
    lanes = lambda: pl.BlockSpec((nj, _S5_SW), lambda i: (0, i))
    yi, vre, vim = pl.pallas_call(
        _s5_in_kernel,
        grid=(p,),
        in_specs=[pair3((nj, _S5_PW)), pair3((_S5_PW, _S5_PW + 2 * _S5_SW)), pair3((1, _S5_PW))],
        out_specs=[pair3((nj, _S5_PW)), lanes(), lanes()],
        out_shape=[jax.ShapeDtypeStruct((p, nj, _S5_PW), F32),
                   jax.ShapeDtypeStruct((nj, p * _S5_SW), F32),
                   jax.ShapeDtypeStruct((nj, p * _S5_SW), F32)],
        compiler_params=_cparams(("parallel",)),
        name="s5_in",
    )(up, wcat, dt)
    sw = p * _S5_SW
    tile = (8, sw // 8)
    lb = tile[1] // 2
    sblk = lambda: pl.BlockSpec((nj, 8, lb), lambda i: (0, 0, i))
    ablk = lambda: pl.BlockSpec((8, lb), lambda i: (0, i))
    xre, xim = pl.pallas_call(
        functools.partial(_s5_scan_kernel, nsteps=nj),
        grid=(2,),
        in_specs=[sblk(), sblk(), ablk(), ablk()],
        out_specs=[sblk(), sblk()],
        out_shape=[jax.ShapeDtypeStruct((nj,) + tile, F32)] * 2,
        compiler_params=_cparams(("parallel",)),
        name="s5_scan",
    )(vre.reshape((nj,) + tile), vim.reshape((nj,) + tile), alr.reshape(tile), ali.reshape(tile))
    yp = pl.pallas_call(
        _s5_out_kernel,
        grid=(p,),
        in_specs=[pair3((nj, _S5_PW)), lanes(), lanes(), pair3((2 * _S5_SW, _S5_PW))],
        out_specs=pair3((nj, _S5_PW)),
        out_shape=jax.ShapeDtypeStruct((p, nj, _S5_PW), F32),
        compiler_params=_cparams(("parallel",)),
        name="s5_out",
    )(yi, xre.reshape(nj, sw), xim.reshape(nj, sw), ws2o)
    y = yp.reshape(p, nj, S5_L, 2 * S5_CH).transpose(1, 2, 0, 3).reshape(t, S5_W)
    tm = min(t, 1024)
    return pl.pallas_call(
        _glu_kernel,
        grid=(t // tm,),
        in_specs=[pl.BlockSpec((tm, S5_W), lambda i: (i, 0)),
                  pl.BlockSpec((S5_W, S5_W), lambda i: (0, 0)),
                  pl.BlockSpec((1, S5_W), lambda i: (0, 0))],
        out_specs=pl.BlockSpec((tm, S5_W), lambda i: (i, 0)),
        out_shape=jax.ShapeDtypeStruct((t, S5_W), BF16),
        compiler_params=_cparams(("parallel",)),
        name="s5_glu",
    )(y, w_glu.astype(BF16), b_glu.astype(F32).reshape(1, S5_W))


def _lower_bounds_kernel(x_ref, o_ref):
    x = x_ref[...]
    e = jnp.exp(x - jnp.max(x, axis=0, keepdims=True))
    p = e / jnp.sum(e, axis=0, keepdims=True)
    run = p[0:1]
    rows = [run - p[0:1]]
    for i in range(1, x.shape[0]):
        run = run + p[i:i + 1]
        rows.append(run - p[0:1])
    o_ref[...] = jnp.concatenate(rows, axis=0)


def lower_bounds(logits):
    return pl.pallas_call(
        _lower_bounds_kernel,
        out_shape=jax.ShapeDtypeStruct(logits.shape, F32),
        name="hgrn_lower_bounds",
    )(logits.astype(F32))


def _out_proj_kernel(h_ref, oa_ref, ob_ref, yc_ref, w1_ref, w2_ref, w3_ref, o_ref):
    o_ref[...] = (h_ref[...]
                  + jnp.dot(oa_ref[...], w1_ref[0].astype(BF16), preferred_element_type=F32)
                  + jnp.dot(ob_ref[...], w2_ref[0].astype(BF16), preferred_element_type=F32)
                  + jnp.dot(yc_ref[...], w3_ref[0].astype(BF16), preferred_element_type=F32))


def out_proj(h, oa, ob, yc, w, layer, *, tm, tn):
    m, n = h.shape
    hw = MIX_HEADS_W
    return pl.pallas_call(
        _out_proj_kernel,
        grid=(m // tm, n // tn),
        in_specs=[pl.BlockSpec((tm, tn), lambda i, j: (i, j)),
                  pl.BlockSpec((tm, hw), lambda i, j: (i, 0)),
                  pl.BlockSpec((tm, hw), lambda i, j: (i, 0)),
                  pl.BlockSpec((tm, S5_W), lambda i, j: (i, 0)),
                  pl.BlockSpec((1, hw, tn), lambda i, j: (layer, 0, j)),
                  pl.BlockSpec((1, hw, tn), lambda i, j: (layer, 1, j)),
                  pl.BlockSpec((1, S5_W, tn), lambda i, j: (layer, 2 * hw // S5_W, j))],
        out_specs=pl.BlockSpec((tm, tn), lambda i, j: (i, j)),
        out_shape=jax.ShapeDtypeStruct((m, n), F32),
        compiler_params=_cparams(("parallel", "arbitrary")),
        name="out_proj",
    )(h, oa, ob, yc, w, w, w)


def _xattn_kernel(h_ref, nw_ref, wq_ref, kv_ref, wo_ref, o_ref, wqb_ref, wob_ref):
    @pl.when(pl.program_id(0) == 0)
    def _():
        wqb_ref[...] = wq_ref[0].astype(BF16)
        wob_ref[...] = wo_ref[0].astype(BF16)

    h = h_ref[...]
    q = jnp.dot(_rms(h, nw_ref[...]).astype(BF16), wqb_ref[...], preferred_element_type=F32)
    kv = kv_ref[...]
    outs = []
    for hd in range(XA_HEADS):
        lo, hi = hd * XA_DH, (hd + 1) * XA_DH
        s = _dot_nt(q[:, lo:hi], kv[:, lo:hi]) * (XA_DH ** -0.5)
        e = jnp.exp(s - jnp.max(s, axis=-1, keepdims=True))
        p = e / jnp.sum(e, axis=-1, keepdims=True)
        outs.append(_dot(p, kv[:, XA_W + lo:XA_W + hi]))
    o = jnp.concatenate(outs, axis=1)
    o_ref[...] = h + _dot(o, wob_ref[...])


def xattn(h, nw, wq, kv, wo, layer, *, tm):
    m, d = h.shape
    nw = nw.reshape(1, d)
    return pl.pallas_call(
        _xattn_kernel,
        grid=(m // tm,),
        in_specs=[pl.BlockSpec((tm, d), lambda i: (i, 0)),
                  pl.BlockSpec((1, d), lambda i: (0, 0)),
                  pl.BlockSpec((1, d, XA_W), lambda i: (layer, 0, 0)),
                  pl.BlockSpec(kv.shape, lambda i: (0, 0)),
                  pl.BlockSpec((1, XA_W, d), lambda i: (layer, 0, 0))],
        out_specs=pl.BlockSpec((tm, d), lambda i: (i, 0)),
        out_shape=jax.ShapeDtypeStruct((m, d), F32),
        scratch_shapes=[pltpu.VMEM((d, XA_W), BF16), pltpu.VMEM((XA_W, d), BF16)],
        compiler_params=_cparams(("arbitrary",)),
        name="xattn",
    )(h, nw, wq, kv, wo)


def _ffn_kernel(te_ref, nv_ref, x_ref, nw_ref, wg_ref, wu_ref, wd_ref, o_ref, xn_ref, acc_ref, *, norm, residual, nf):
    i = pl.program_id(0)
    f = pl.program_id(1)

    @pl.when(i < nv_ref[0])
    def _():
        @pl.when(f == 0)
        def _():
            x = x_ref[...]
            xn_ref[...] = (_rms(x, nw_ref[...]) if norm else x).astype(BF16)
            acc_ref[...] = jnp.zeros_like(acc_ref)

        xn = xn_ref[...]
        g = jnp.dot(xn, wg_ref[0].astype(BF16), preferred_element_type=F32)
        u = jnp.dot(xn, wu_ref[0].astype(BF16), preferred_element_type=F32)
        acc_ref[...] += jnp.dot((_silu(g) * u).astype(BF16), wd_ref[0].astype(BF16), preferred_element_type=F32)

        @pl.when(f == nf - 1)
        def _():
            o_ref[...] = (acc_ref[...] + x_ref[...]) if residual else acc_ref[...]

    @pl.when((i >= nv_ref[0]) & (f == nf - 1))
    def _():
        o_ref[...] = jnp.zeros_like(o_ref)


def ffn(x, nw, wg, wu, wd, tile_expert, nvalid, *, tm, tf, norm, residual):
    m, d = x.shape
    dff = wg.shape[2]
    nf = dff // tf

    def row(i, f, te, nv):
        return (jnp.minimum(i, nv[0] - 1), 0)

    def wcol(i, f, te, nv):
        return (te[jnp.minimum(i, nv[0] - 1)], 0, jnp.where(i < nv[0], f, nf - 1))

    def wrow(i, f, te, nv):
        return (te[jnp.minimum(i, nv[0] - 1)], jnp.where(i < nv[0], f, nf - 1), 0)

    return pl.pallas_call(
        functools.partial(_ffn_kernel, norm=norm, residual=residual, nf=nf),
        grid_spec=pltpu.PrefetchScalarGridSpec(
            num_scalar_prefetch=2,
            grid=(m // tm, nf),
            in_specs=[pl.BlockSpec((tm, d), row),
                      pl.BlockSpec((1, d), lambda i, f, te, nv: (0, 0)),
                      pl.BlockSpec((1, d, tf), wcol),
                      pl.BlockSpec((1, d, tf), wcol),
                      pl.BlockSpec((1, tf, d), wrow)],
            out_specs=pl.BlockSpec((tm, d), lambda i, f, te, nv: (i, 0)),
            scratch_shapes=[pltpu.VMEM((tm, d), BF16), pltpu.VMEM((tm, d), F32)]),
        out_shape=jax.ShapeDtypeStruct((m, d), F32),
        compiler_params=_cparams(("arbitrary", "arbitrary")),
        name="ffn",
    )(tile_expert, nvalid, x, nw.reshape(1, d), wg, wu, wd)


def _router_kernel(h_ref, nw_ref, rw_ref, hn_ref, info_ref, cnt_ref, carry_ref, tri_ref, *, tm):
    @pl.when(pl.program_id(0) == 0)
    def _():
        carry_ref[...] = jnp.zeros_like(carry_ref)
        r = lax.broadcasted_iota(jnp.int32, (tm, tm), 0)
        c = lax.broadcasted_iota(jnp.int32, (tm, tm), 1)
        tri_ref[...] = (c < r).astype(BF16)

    hn = _rms(h_ref[...], nw_ref[...])
    hn_ref[...] = hn
    logits = _dot3(hn, rw_ref[...])
    lane = lax.broadcasted_iota(jnp.int32, (tm, LANES), 1).astype(F32)
    neg = -jnp.inf
    lg = jnp.where(lane < N_EXPERTS, logits, neg)
    m1 = jnp.max(lg, axis=1, keepdims=True)
    i1 = jnp.min(jnp.where(lg == m1, lane, float(LANES)), axis=1, keepdims=True)
    lg2 = jnp.where(lane == i1, neg, lg)
    m2 = jnp.max(lg2, axis=1, keepdims=True)
    i2 = jnp.min(jnp.where(lg2 == m2, lane, float(LANES)), axis=1, keepdims=True)
    e2 = jnp.exp(m2 - m1)
    g1 = 1.0 / (1.0 + e2)
    g2 = e2 / (1.0 + e2)
    cnt = jnp.where((lane == i1) | (lane == i2), 1.0, 0.0)
    carry = carry_ref[...]
    before = jnp.dot(tri_ref[...], cnt.astype(BF16), preferred_element_type=F32) + carry
    r1 = jnp.sum(jnp.where(lane == i1, before, 0.0), axis=1, keepdims=True)
    r2 = jnp.sum(jnp.where(lane == i2, before, 0.0), axis=1, keepdims=True)
    carry = carry + jnp.sum(cnt, axis=0, keepdims=True)
    carry_ref[...] = carry
    cnt_ref[...] = jnp.broadcast_to(carry, cnt_ref.shape)
    info = jnp.zeros((tm, LANES), F32)
    for ln, val in enumerate((i1, i2, r1, r2, g1, g2)):
        info = jnp.where(lane == ln, val, info)
    info_ref[...] = info


def router(h, nw, rw, *, tm):
    m, d = h.shape
    rwp = jnp.zeros((d, LANES), F32).at[:, :N_EXPERTS].set(rw.astype(F32))
    return pl.pallas_call(
        functools.partial(_router_kernel, tm=tm),
        grid=(m // tm,),
        in_specs=[pl.BlockSpec((tm, d), lambda i: (i, 0)),
                  pl.BlockSpec((1, d), lambda i: (0, 0)),
                  pl.BlockSpec((d, LANES), lambda i: (0, 0))],
        out_specs=[pl.BlockSpec((tm, d), lambda i: (i, 0)),
                   pl.BlockSpec((tm, LANES), lambda i: (i, 0)),
                   pl.BlockSpec((8, LANES), lambda i: (0, 0))],
        out_shape=[jax.ShapeDtypeStruct((m, d), F32),
                   jax.ShapeDtypeStruct((m, LANES), F32),
                   jax.ShapeDtypeStruct((8, LANES), F32)],
        scratch_shapes=[pltpu.VMEM((1, LANES), F32), pltpu.VMEM((tm, tm), BF16)],
        compiler_params=_cparams(("arbitrary",)),
        name="moe_router",
    )(h, nw.reshape(1, d), rwp)


def _dispatch_kernel(dest_ref, hn_ref, init_ref, xb_ref, sem, *, tm):
    del init_ref
    base = pl.program_id(0) * tm

    def row_copy(t, d):
        return pltpu.make_async_copy(hn_ref.at[pl.ds(t, 1)], xb_ref.at[pl.ds(d, 1)], sem)

    def issue(t, c):
        row_copy(t, dest_ref[2 * (base + t)]).start()
        row_copy(t, dest_ref[2 * (base + t) + 1]).start()
        return c

    def drain(t, c):
        row_copy(0, 0).wait()
        row_copy(0, 0).wait()
        return c

    lax.fori_loop(0, tm, issue, 0)
    lax.fori_loop(0, tm, drain, 0)


def dispatch(hn, dest, rows, *, tm):
    m, d = hn.shape
    return pl.pallas_call(
        functools.partial(_dispatch_kernel, tm=tm),
        grid_spec=pltpu.PrefetchScalarGridSpec(
            num_scalar_prefetch=1,
            grid=(m // tm,),
            in_specs=[pl.BlockSpec((tm, d), lambda i, dst: (i, 0)),
                      pl.BlockSpec(memory_space=pl.ANY)],
            out_specs=pl.BlockSpec(memory_space=pl.ANY),
            scratch_shapes=[pltpu.SemaphoreType.DMA(())]),
        out_shape=jax.ShapeDtypeStruct((rows, d), F32),
        input_output_aliases={2: 0},
        compiler_params=_cparams(("arbitrary",)),
        name="moe_dispatch",
    )(dest, hn, jnp.zeros((rows, d), F32))


def _combine_kernel(dest_ref, h_ref, info_ref, yb_ref, o_ref, buf_ref, sem, *, tm):
    base = pl.program_id(0) * tm

    def row_copy(t, d, slot):
        return pltpu.make_async_copy(yb_ref.at[pl.ds(d, 1)], buf_ref.at[slot, pl.ds(t, 1)], sem)

    def issue(t, c):
        row_copy(t, dest_ref[2 * (base + t)], 0).start()
        row_copy(t, dest_ref[2 * (base + t) + 1], 1).start()
        return c

    def drain(t, c):
        row_copy(0, 0, 0).wait()
        row_copy(0, 0, 1).wait()
        return c

    lax.fori_loop(0, tm, issue, 0)
    lax.fori_loop(0, tm, drain, 0)
    info = info_ref[...]
    o_ref[...] = h_ref[...] + (info[:, 4:5] * buf_ref[0] + info[:, 5:6] * buf_ref[1])


def combine(h, info, yb, dest, *, tm):
    m, d = h.shape
    return pl.pallas_call(
        functools.partial(_combine_kernel, tm=tm),
        grid_spec=pltpu.PrefetchScalarGridSpec(
            num_scalar_prefetch=1,
            grid=(m // tm,),
            in_specs=[pl.BlockSpec((tm, d), lambda i, dst: (i, 0)),
                      pl.BlockSpec((tm, LANES), lambda i, dst: (i, 0)),
                      pl.BlockSpec(memory_space=pl.ANY)],
            out_specs=pl.BlockSpec((tm, d), lambda i, dst: (i, 0)),
            scratch_shapes=[pltpu.VMEM((2, tm, d), F32), pltpu.SemaphoreType.DMA(())]),
        out_shape=jax.ShapeDtypeStruct((m, d), F32),
        compiler_params=_cparams(("arbitrary",)),
        name="moe_combine",
    )(dest, h, info, yb)


def moe_layer(h, nw, rw, wg, wu, wd, expert0):
    m, d = h.shape
    tm = MOE_TM
    ntiles = m * 2 // tm + N_EXPERTS
    hn, info, cnt = router(h, nw, rw, tm=tm)
    counts = cnt[0, :N_EXPERTS].astype(jnp.int32)
    seg = (counts + tm - 1) // tm * tm
    seg_end = jnp.cumsum(seg)
    seg_start = seg_end - seg
    ids = info[:, 0:2].astype(jnp.int32)
    dest = (seg_start[ids] + info[:, 2:4].astype(jnp.int32)).reshape(2 * m)
    tile_start = jnp.arange(ntiles, dtype=jnp.int32) * tm
    tile_expert = expert0 + jnp.minimum(
        jnp.sum((seg_end[None, :] <= tile_start[:, None]).astype(jnp.int32), axis=1), N_EXPERTS - 1)
    nvalid = (seg_end[-1:] // tm).astype(jnp.int32)
    xb = dispatch(hn, dest, ntiles * tm, tm=256)
    yb = ffn(xb, nw, wg, wu, wd, tile_expert, nvalid, tm=tm, tf=FFN_TF, norm=False, residual=False)
    return combine(h, info, yb, dest, tm=256)


def _final_norm_kernel(x_ref, w_ref, o_ref):
    o_ref[...] = _rms(x_ref[...], w_ref[...])


def final_norm(h, w, *, tm):
    m, d = h.shape
    return pl.pallas_call(
        _final_norm_kernel,
        grid=(m // tm,),
        in_specs=[pl.BlockSpec((tm, d), lambda i: (i, 0)), pl.BlockSpec((1, d), lambda i: (0, 0))],
        out_specs=pl.BlockSpec((tm, d), lambda i: (i, 0)),
        out_shape=jax.ShapeDtypeStruct((m, d), F32),
        compiler_params=_cparams(("parallel",)),
        name="final_norm",
    )(h, w.reshape(1, d))


def _pack_w_in_tail(w_in):
    gates = w_in[:, :, IN_MAIN:IN_MAIN + 2 * N_HEADS]
    u = w_in[:, :, IN_MAIN + 2 * N_HEADS:]
    pad = jnp.zeros(w_in.shape[:2] + (IN_TAIL - S5_W - 2 * N_HEADS,), w_in.dtype)
    return jnp.concatenate([u, gates, pad], axis=2).astype(BF16)


def kernel(x, mem, norm_mix, w_in, w_out, hgrn_lb_logits, hgrn_norm, gdn_conv, gdn_a_log, gdn_dt_bias, gdn_norm, s5_a_re, s5_a_im, s5_b_re, s5_b_im, s5_c_re, s5_c_im, s5_d, s5_log_step, s5_w_glu, s5_b_glu, norm_cross, norm_mem, xa_wq, xa_wk, xa_wv, xa_wo, norm_ffn, ffn_w_gate, ffn_w_up, ffn_w_down, moe_router, moe_w_gate, moe_w_up, moe_w_down, norm_final):
    bsz, t, d = x.shape
    assert bsz == 1
    depth = w_in.shape[0]
    h = x[0].astype(F32)
    memx = mem[0].astype(F32)
    lbs = lower_bounds(hgrn_lb_logits)
    tb = min(t, 512)
    tmr = min(t, 1024)
    dense_tiles = t // MOE_TM
    w_tail = _pack_w_in_tail(w_in)
    w_kv = jnp.concatenate([xa_wk, xa_wv], axis=2)
    moe_wg, moe_wu, moe_wd = (w.reshape((-1,) + w.shape[2:]) for w in (moe_w_gate, moe_w_up, moe_w_down))
    for layer in range(depth):
        proj = norm_matmul(h, norm_mix[layer], w_in, layer, IN_MAIN, tm=tmr, tn=512)
        tail = norm_matmul(h, norm_mix[layer], w_tail, layer, IN_TAIL, tm=tmr, tn=IN_TAIL)
        oa = hgrn_mixer(proj, lbs[layer], hgrn_norm[layer], tb=tb, hpb=HEADS_PER_BODY)
        ob = gdn_mixer(proj, tail, S5_W // LANES, gdn_conv[layer].astype(F32), gdn_a_log[layer],
                       gdn_dt_bias[layer], gdn_norm[layer], tb=tb, hpb=HEADS_PER_BODY)
        yc = s5_mixer(tail[:, :S5_W], s5_a_re[layer], s5_a_im[layer], s5_b_re[layer],
                      s5_b_im[layer], s5_c_re[layer], s5_c_im[layer], s5_d[layer], s5_log_step[layer],
                      s5_w_glu[layer], s5_b_glu[layer])
        h = out_proj(h, oa, ob, yc, w_out, layer, tm=tmr, tn=512)
        kv = norm_matmul(memx, norm_mem[layer], w_kv, layer, 2 * XA_W, tm=memx.shape[0], tn=512, out_dtype=BF16)
        h = xattn(h, norm_cross[layer], xa_wq, kv, xa_wo, layer, tm=MOE_TM)
        i = layer // 2
        if layer % 2 == 0:
            h = ffn(h, norm_ffn[layer], ffn_w_gate, ffn_w_up, ffn_w_down, jnp.full((dense_tiles,), i, jnp.int32),
                    jnp.full((1,), dense_tiles, jnp.int32), tm=MOE_TM, tf=FFN_TF, norm=True, residual=True)
        else:
            h = moe_layer(h, norm_ffn[layer], moe_router[i], moe_wg, moe_wu, moe_wd, i * N_EXPERTS)
    return final_norm(h, norm_final, tm=tmr)[None].astype(x.dtype)
```

```python
import functools
import math

import jax
import jax.numpy as jnp
from jax import lax
from jax.experimental import pallas as pl
from jax.experimental.pallas import tpu as pltpu

F32 = jnp.float32
BF16 = jnp.bfloat16

D_MODEL = 2048
N_HEADS = 6
HEAD_DIM = 128
CHUNK = 64
SUB = 16
CONV_K = 4
S5_GROUPS = 32
S5_CH = 16
S5_STATE = 64
S5_W = S5_GROUPS * S5_CH
S5_L = 16
S5_PAIRS = S5_GROUPS // 2
XA_HEADS = 4
XA_DH = 128
XA_W = XA_HEADS * XA_DH
D_FF = 5632
N_EXPERTS = 8
NORM_EPS = 1e-6
LB_FLOOR = 1e-30
LANES = 128
MIX_HEADS_W = N_HEADS * HEAD_DIM
IN_MAIN = 8 * MIX_HEADS_W
IN_TAIL = S5_W + LANES
HEADS_PER_BODY = 6
MOE_TM = 512
MOE_TF = 512
DENSE_TM = 1024
DENSE_TF = 256
VMEM_LIMIT = 56 * 1024 * 1024


def _cparams(sem, vmem=VMEM_LIMIT):
    return pltpu.CompilerParams(dimension_semantics=sem, vmem_limit_bytes=vmem)


def _dot(a, b):
    return jnp.dot(a.astype(BF16), b.astype(BF16), preferred_element_type=F32)


def _dot_nt(a, b):
    return lax.dot_general(a.astype(BF16), b.astype(BF16), (((1,), (1,)), ((), ())),
                           preferred_element_type=F32)


def _dot_tn(a, b):
    return lax.dot_general(a.astype(BF16), b.astype(BF16), (((0,), (0,)), ((), ())),
                           preferred_element_type=F32)


def _split3(x):
    hi = x.astype(BF16)
    r1 = x - hi.astype(F32)
    mid = r1.astype(BF16)
    lo = (r1 - mid.astype(F32)).astype(BF16)
    return hi, mid, lo


def _dot_exact_lhs(m_bf16, x):
    hi, mid, lo = _split3(x)
    return (jnp.dot(m_bf16, hi, preferred_element_type=F32)
            + jnp.dot(m_bf16, mid, preferred_element_type=F32)
            + jnp.dot(m_bf16, lo, preferred_element_type=F32))


def _rms(x, w):
    ms = jnp.mean(x * x, axis=-1, keepdims=True)
    return x * lax.rsqrt(ms + NORM_EPS) * w


def _silu(x):
    return x * jax.nn.sigmoid(x)


def _norm_matmul_kernel(x_ref, nw_ref, w_ref, o_ref, xn_ref):
    @pl.when(pl.program_id(1) == 0)
    def _():
        xn_ref[...] = _rms(x_ref[...], nw_ref[...]).astype(BF16)

    o_ref[...] = jnp.dot(xn_ref[...], w_ref[0].astype(BF16), preferred_element_type=F32).astype(o_ref.dtype)


def norm_matmul(x, nw, w, layer, n, *, tm, tn, out_dtype=F32):
    m, k = x.shape
    return pl.pallas_call(
        _norm_matmul_kernel,
        grid=(m // tm, n // tn),
        in_specs=[pl.BlockSpec((tm, k), lambda i, j: (i, 0)),
                  pl.BlockSpec((1, k), lambda i, j: (0, 0)),
                  pl.BlockSpec((1, k, tn), lambda i, j: (layer, 0, j))],
        out_specs=pl.BlockSpec((tm, tn), lambda i, j: (i, j)),
        out_shape=jax.ShapeDtypeStruct((m, n), out_dtype),
        scratch_shapes=[pltpu.VMEM((tm, k), BF16)],
        compiler_params=_cparams(("parallel", "arbitrary")),
        name="norm_matmul",
    )(x, nw.reshape(1, k), w)


def _chunk_masks():
    r = lax.broadcasted_iota(jnp.int32, (CHUNK, CHUNK), 0)
    c = lax.broadcasted_iota(jnp.int32, (CHUNK, CHUNK), 1)
    return r, c


def _round_robin(stages):
    live = list(stages)
    while live:
        nxt = []
        for g in live:
            try:
                next(g)
                nxt.append(g)
            except StopIteration:
                pass
        live = nxt


def _hgrn_kernel(q_ref, f_ref, i_ref, g_ref, lb_ref, nw_ref, o_ref, st_ref, *, nchunk, hpb):
    @pl.when(pl.program_id(1) == 0)
    def _():
        st_ref[...] = jnp.zeros_like(st_ref)

    nsub = CHUNK // SUB
    nw = nw_ref[...]
    r, c = _chunk_masks()
    rb = lax.shift_right_logical(r, 4)
    cb = lax.shift_right_logical(c, 4)
    mats = [(c <= r)] + [(c < SUB * (j + 1)) for j in range(nsub)]
    pmat = jnp.concatenate([m.astype(BF16) for m in mats], axis=0)
    diag_mask = (rb == cb) & (c <= r)
    rcol = lax.broadcasted_iota(jnp.int32, (CHUNK, 1), 0)
    rbcol = lax.shift_right_logical(rcol, 4)

    def head_chunk(sl, hd):
        hs = slice(hd * HEAD_DIM, (hd + 1) * HEAD_DIM)
        lb = lb_ref[:, hs]
        lbf = jnp.maximum(lb, LB_FLOOR)
        oml = 1.0 - lb
        fa = f_ref[sl, hs]
        qa = q_ref[sl, hs]
        v = i_ref[sl, hs]
        ga = g_ref[sl, hs]
        logf = jnp.log(lbf + oml * jax.nn.sigmoid(fa))
        k = oml * jax.nn.sigmoid(-fa)
        q = _silu(qa) * (HEAD_DIM ** -0.5)
        sums = _dot_exact_lhs(pmat, logf)
        yield
        cum = sums[0:CHUNK]
        ends = [sums[CHUNK * (j + 1):CHUNK * (j + 2)] for j in range(nsub)]
        last = ends[nsub - 1]
        base = jnp.zeros_like(cum)
        endv = ends[0]
        for j in range(1, nsub):
            base = jnp.where(rbcol >= j, ends[j - 1], base)
            endv = jnp.where(rbcol >= j, ends[j], endv)
        st = st_ref[hd]
        o = _dot_nt(q * jnp.exp(cum), st)
        st_new = st * jnp.exp(last[0:1, :]) + _dot_tn(v, k * jnp.exp(last - cum))
        attn = jnp.where(diag_mask, _dot_nt(q * jnp.exp(cum - base), k * jnp.exp(base - cum)), 0.0)
        yield
        ko = k * jnp.exp(endv - cum)
        for j in range(nsub - 1):
            below = rbcol > j
            qo = jnp.where(below, q * jnp.exp(jnp.where(below, cum - ends[j], 0.0)), 0.0)
            kj = jnp.where(rbcol == j, ko, 0.0)
            attn = attn + _dot_nt(qo, kj)
        yield
        o = o + _dot(attn, v)
        yield
        st_ref[hd] = st_new
        o_ref[sl, hs] = (_rms(o, nw) * _silu(ga)).astype(o_ref.dtype)

    def body(ci, carry):
        sl = pl.ds(pl.multiple_of(ci * CHUNK, CHUNK), CHUNK)
        _round_robin([head_chunk(sl, hd) for hd in range(hpb)])
        return carry

    lax.fori_loop(0, nchunk, body, 0)


def hgrn_mixer(proj, lb, nw, *, tb, hpb):
    t = proj.shape[0]
    nchunk = tb // CHUNK
    ngrp = N_HEADS // hpb
    w = hpb * HEAD_DIM

    def col(part):
        return pl.BlockSpec((tb, w), lambda g, i, part=part: (i, part * ngrp + g))

    return pl.pallas_call(
        functools.partial(_hgrn_kernel, nchunk=nchunk, hpb=hpb),
        grid=(ngrp, t // tb),
        in_specs=[col(0), col(1), col(2), col(3),
                  pl.BlockSpec((1, w), lambda g, i: (0, g)),
                  pl.BlockSpec((1, HEAD_DIM), lambda g, i: (0, 0))],
        out_specs=pl.BlockSpec((tb, w), lambda g, i: (i, g)),
        out_shape=jax.ShapeDtypeStruct((t, MIX_HEADS_W), BF16),
        scratch_shapes=[pltpu.VMEM((hpb, HEAD_DIM, HEAD_DIM), F32)],
        compiler_params=_cparams(("parallel", "arbitrary")),
        name="hgrn_mixer",
    )(proj, proj, proj, proj, lb.reshape(1, MIX_HEADS_W), nw.reshape(1, HEAD_DIM))


def _lane_pick(x, lane_ids, idx):
    return jnp.sum(jnp.where(lane_ids == idx, x, 0.0), axis=1, keepdims=True)


def _gdn_kernel(q_ref, k_ref, v_ref, z_ref, ab_ref, cq_ref, ck_ref, cv_ref, alog_ref, dtb_ref, nw_ref,
                o_ref, st_ref, xq_ref, xk_ref, xv_ref, *, nchunk, tb, hpb):
    head0 = pl.program_id(0) * hpb
    tail = 8
    width = hpb * HEAD_DIM

    @pl.when(pl.program_id(1) == 0)
    def _():
        st_ref[...] = jnp.zeros_like(st_ref)
        for xr in (xq_ref, xk_ref, xv_ref):
            xr[0:tail, :] = jnp.zeros((tail, width), F32)

    for src, xr, cw in ((q_ref, xq_ref, cq_ref), (k_ref, xk_ref, ck_ref), (v_ref, xv_ref, cv_ref)):
        xr[tail:tail + tb, :] = src[...]
        acc = jnp.zeros((tb, width), F32)
        for j in range(CONV_K):
            acc = acc + cw[j:j + 1, :] * xr[pl.ds(tail - (CONV_K - 1) + j, tb), :]
        new_tail = xr[tb:tb + tail, :]
        xr[tail:tail + tb, :] = _silu(acc)
        xr[0:tail, :] = new_tail

    nw = nw_ref[...]
    r, c = _chunk_masks()
    incl = c <= r
    strict = c < r
    eye = c == r
    tri = incl.astype(BF16)
    ones = jnp.ones((CHUNK, CHUNK), BF16)
    lane = lax.broadcasted_iota(jnp.int32, (CHUNK, LANES), 1)
    neg_a = -jnp.exp(alog_ref[...])
    dtb = dtb_ref[...]

    def head_chunk(sl, slx, hd, cum_all, beta_all):
        hs = slice(hd * HEAD_DIM, (hd + 1) * HEAD_DIM)
        qc = xq_ref[slx, hs]
        kc = xk_ref[slx, hs]
        v = xv_ref[slx, hs]
        q = qc * lax.rsqrt(jnp.sum(qc * qc, axis=-1, keepdims=True) + 1e-6) * (HEAD_DIM ** -0.5)
        k = kc * lax.rsqrt(jnp.sum(kc * kc, axis=-1, keepdims=True) + 1e-6)
        cum = _lane_pick(cum_all, lane, head0 + hd)
        beta = _lane_pick(beta_all, lane, head0 + hd + N_HEADS)
        cum_b = jnp.broadcast_to(cum, (CHUNK, CHUNK))
        cum_row = _dot_exact_lhs(ones, jnp.where(eye, cum_b, 0.0))
        kk = _dot_nt(k, k)
        qk = _dot_nt(q, k)
        yield
        gam = jnp.where(incl, jnp.exp(jnp.where(incl, cum_b - cum_row, 0.0)), 0.0)
        last = cum[CHUNK - 1:CHUNK, :]
        ecum = jnp.exp(cum)
        a = jnp.where(strict, kk * gam, 0.0) * beta
        st = st_ref[hd]
        o = _dot(q * ecum, st)
        eyef = eye.astype(F32)
        tinv = eyef - a
        pw = a
        for _ in range(int(math.log2(CHUNK)) - 1):
            pw = _dot(pw, pw)
            yield
            tinv = tinv + _dot(tinv, pw)
            yield
        rhs = jnp.concatenate([v * beta, k * (beta * ecum)], axis=1)
        sol = _dot(tinv, rhs)
        yield
        u = sol[:, :HEAD_DIM]
        w = sol[:, HEAD_DIM:]
        v_new = u - _dot(w, st)
        yield
        o = o + _dot(qk * gam, v_new)
        st_ref[hd] = st * jnp.exp(last) + _dot_tn(k * jnp.exp(last - cum), v_new)
        yield
        o_ref[sl, hs] = (_rms(o, nw) * _silu(z_ref[sl, hs])).astype(o_ref.dtype)

    def body(ci, carry):
        off = pl.multiple_of(ci * CHUNK, CHUNK)
        sl = pl.ds(off, CHUNK)
        slx = pl.ds(off + tail, CHUNK)
        gates = ab_ref[sl, :]
        cum_all = _dot_exact_lhs(tri, neg_a * jax.nn.softplus(gates + dtb))
        beta_all = jax.nn.sigmoid(gates)
        _round_robin([head_chunk(sl, slx, hd, cum_all, beta_all) for hd in range(hpb)])
        return carry

    lax.fori_loop(0, nchunk, body, 0)


def gdn_mixer(proj, gates, gate_blk, conv_w, a_log, dt_bias, nw, *, tb, hpb):
    t = proj.shape[0]
    nchunk = tb // CHUNK
    ngrp = N_HEADS // hpb
    w = hpb * HEAD_DIM

    def col(part):
        return pl.BlockSpec((tb, w), lambda g, i, part=part: (i, part * ngrp + g))

    def cw(part):
        return pl.BlockSpec((CONV_K, w), lambda g, i, part=part: (0, part * ngrp + g))

    def row_pad(vec):
        return jnp.zeros((1, LANES), F32).at[0, :N_HEADS].set(vec.astype(F32))

    return pl.pallas_call(
        functools.partial(_gdn_kernel, nchunk=nchunk, tb=tb, hpb=hpb),
        grid=(ngrp, t // tb),
        in_specs=[col(4), col(5), col(6), col(7),
                  pl.BlockSpec((tb, LANES), lambda g, i: (i, gate_blk)),
                  cw(0), cw(1), cw(2),
                  pl.BlockSpec((1, LANES), lambda g, i: (0, 0)),
                  pl.BlockSpec((1, LANES), lambda g, i: (0, 0)),
                  pl.BlockSpec((1, HEAD_DIM), lambda g, i: (0, 0))],
        out_specs=pl.BlockSpec((tb, w), lambda g, i: (i, g)),
        out_shape=jax.ShapeDtypeStruct((t, MIX_HEADS_W), BF16),
        scratch_shapes=[pltpu.VMEM((hpb, HEAD_DIM, HEAD_DIM), F32)]
        + [pltpu.VMEM((tb + 8, w), F32) for _ in range(3)],
        compiler_params=_cparams(("parallel", "arbitrary")),
        name="gdn_mixer",
    )(proj, proj, proj, proj, gates, conv_w, conv_w, conv_w, row_pad(a_log), row_pad(dt_bias),
      nw.reshape(1, HEAD_DIM))


_S5_PW = S5_L * 2 * S5_CH
_S5_SW = 2 * S5_STATE


def _dot3(a, b):
    ah = a.astype(BF16)
    al = (a - ah.astype(F32)).astype(BF16)
    bh = b.astype(BF16)
    bl = (b - bh.astype(F32)).astype(BF16)
    return (jnp.dot(ah, bh, preferred_element_type=F32) + jnp.dot(ah, bl, preferred_element_type=F32)
            + jnp.dot(al, bh, preferred_element_type=F32))


def _s5_tables_kernel(arc_ref, aic_ref, lsc_ref, arr_ref, air_ref, lsr_ref, cre_ref, cim_ref, bre_ref, bim_ref,
                      wcat_ref, ws2o_ref, alr_ref, ali_ref):
    are = jnp.minimum(arc_ref[0], -1e-4)
    aim = aic_ref[0]
    delta = jnp.exp(lsc_ref[0])
    lane = lax.broadcasted_iota(jnp.int32, (1, _S5_PW), 1)
    tau = lax.shift_right_logical(lane, 5).astype(F32)
    mag = jnp.exp(are * delta * tau)
    ang = aim * delta * tau
    lr = mag * jnp.cos(ang)
    li = mag * jnp.sin(ang)
    cre = cre_ref[0]
    cim = cim_ref[0]
    rr0 = jnp.concatenate([lr * cre - li * cim, -(lr * cim + li * cre)], axis=0)
    m1 = jnp.exp(are * delta)
    l1r = m1 * jnp.cos(aim * delta)
    l1i = m1 * jnp.sin(aim * delta)
    lr1 = lr * l1r - li * l1i
    li1 = lr * l1i + li * l1r
    ws2o_ref[0] = jnp.concatenate([lr1 * cre - li1 * cim, -(lr1 * cim + li1 * cre)], axis=0).astype(BF16)

    ar = jnp.minimum(arr_ref[0], -1e-4)
    ai = air_ref[0]
    dl = jnp.exp(lsr_ref[0])
    mb = jnp.exp(ar * dl)
    lbr = mb * jnp.cos(ai * dl)
    lbi = mb * jnp.sin(ai * dl)
    den = ar * ar + ai * ai
    xr = lbr - 1.0
    cr = (xr * ar + lbi * ai) / den
    ci = (lbi * ar - xr * ai) / den
    bre = bre_ref[0]
    bim = bim_ref[0]
    bbr = cr * bre - ci * bim
    bbi = cr * bim + ci * bre
    g = _dot3(jnp.concatenate([bbr, bbi], axis=1), rr0)
    rows_per = 2 * S5_CH
    for s in range(S5_L):
        if s == 0:
            blk = g
        else:
            blk = jnp.where(lane >= rows_per * s, pltpu.roll(g, rows_per * s, axis=1), 0.0)
        wcat_ref[0, rows_per * s:rows_per * (s + 1), 0:_S5_PW] = blk.astype(BF16)
    row = lax.broadcasted_iota(jnp.int32, (_S5_PW, 1), 0)
    mpow = (S5_L - 1 - lax.shift_right_logical(row, 5)).astype(F32)
    pm = jnp.exp(ar * dl * mpow)
    pr = pm * jnp.cos(ai * dl * mpow)
    pi = pm * jnp.sin(ai * dl * mpow)
    bbr_t = jnp.concatenate([bbr] * S5_L, axis=0)
    bbi_t = jnp.concatenate([bbi] * S5_L, axis=0)
    wcat_ref[0, :, _S5_PW:_S5_PW + _S5_SW] = (pr * bbr_t - pi * bbi_t).astype(BF16)
    wcat_ref[0, :, _S5_PW + _S5_SW:] = (pr * bbi_t + pi * bbr_t).astype(BF16)
    ml = jnp.exp(ar * dl * S5_L)
    alr_ref[0] = ml * jnp.cos(ai * dl * S5_L)
    ali_ref[0] = ml * jnp.sin(ai * dl * S5_L)


def s5_tables(a_re, a_im, b_re, b_im, c_re, c_im, log_step):
    p = S5_PAIRS
    ls = jnp.broadcast_to(log_step.astype(F32)[:, None], (S5_GROUPS, S5_STATE))
    same = (jnp.eye(2, dtype=F32) > 0).reshape(1, 2, 2, 1, 1)

    def ctile(cm):
        x = jnp.where(same, cm.astype(F32).reshape(p, 1, 2, S5_CH, S5_STATE), 0.0)
        x = x.transpose(0, 1, 4, 2, 3)[:, :, :, None]
        return jnp.broadcast_to(x, (p, 2, S5_STATE, S5_L, 2, S5_CH)).reshape(p, _S5_SW, _S5_PW)

    def btile(bm):
        x = jnp.where(same, bm.astype(F32).reshape(p, 2, 1, S5_STATE, S5_CH), 0.0)
        return x.transpose(0, 1, 4, 2, 3).reshape(p, 2 * S5_CH, _S5_SW)

    col = lambda v: v.astype(F32).reshape(p, _S5_SW, 1)
    rowv = lambda v: v.astype(F32).reshape(p, 1, _S5_SW)
    spec = lambda shape: pl.BlockSpec((1,) + shape, lambda i: (i, 0, 0))
    return pl.pallas_call(
        _s5_tables_kernel,
        grid=(p,),
        in_specs=[spec((_S5_SW, 1))] * 3 + [spec((1, _S5_SW))] * 3
        + [spec((_S5_SW, _S5_PW))] * 2 + [spec((2 * S5_CH, _S5_SW))] * 2,
        out_specs=[spec((_S5_PW, _S5_PW + 2 * _S5_SW)), spec((2 * _S5_SW, _S5_PW)),
                   spec((1, _S5_SW)), spec((1, _S5_SW))],
        out_shape=[jax.ShapeDtypeStruct((p, _S5_PW, _S5_PW + 2 * _S5_SW), BF16),
                   jax.ShapeDtypeStruct((p, 2 * _S5_SW, _S5_PW), BF16),
                   jax.ShapeDtypeStruct((p, 1, _S5_SW), F32),
                   jax.ShapeDtypeStruct((p, 1, _S5_SW), F32)],
        compiler_params=_cparams(("parallel",)),
        name="s5_tables",
    )(col(a_re), col(a_im), col(ls), rowv(a_re), rowv(a_im), rowv(ls),
      ctile(c_re), ctile(c_im), btile(b_re), btile(b_im))


def _s5_in_kernel(u_ref, wcat_ref, d_ref, yi_ref, vre_ref, vim_ref):
    u = u_ref[0]
    ycat = jnp.dot(u.astype(BF16), wcat_ref[0], preferred_element_type=F32)
    yi_ref[0] = ycat[:, :_S5_PW] + d_ref[0] * u
    vre_ref[...] = ycat[:, _S5_PW:_S5_PW + _S5_SW]
    vim_ref[...] = ycat[:, _S5_PW + _S5_SW:]


def _s5_scan_kernel(vre_ref, vim_ref, alr_ref, ali_ref, xre_ref, xim_ref, *, nsteps):
    ar = alr_ref[...]
    ai = ali_ref[...]

    def body(j, carry):
        xr, xi = carry
        xre_ref[j] = xr
        xim_ref[j] = xi
        return (ar * xr - ai * xi + vre_ref[j], ar * xi + ai * xr + vim_ref[j])

    zero = jnp.zeros(ar.shape, F32)
    lax.fori_loop(0, nsteps, body, (zero, zero))


def _gelu_tanh(x):
    return 0.5 * x * (1.0 + jnp.tanh(math.sqrt(2.0 / math.pi) * (x + 0.044715 * (x * x * x))))


def _s5_out_kernel(yi_ref, xre_ref, xim_ref, ws2o_ref, y_ref):
    w = ws2o_ref[0]
    y = (yi_ref[0] + jnp.dot(xre_ref[...].astype(BF16), w[:_S5_SW], preferred_element_type=F32)
         + jnp.dot(xim_ref[...].astype(BF16), w[_S5_SW:], preferred_element_type=F32))
    y_ref[0] = _gelu_tanh(y)


def _glu_kernel(y_ref, w_ref, b_ref, o_ref):
    y = y_ref[...]
    z = jnp.dot(y.astype(BF16), w_ref[...], preferred_element_type=F32) + b_ref[...]
    o_ref[...] = (y * jax.nn.sigmoid(z)).astype(o_ref.dtype)


def s5_mixer(u, a_re, a_im, b_re, b_im, c_re, c_im, d_skip, log_step, w_glu, b_glu):
    t = u.shape[0]
    nj = t // S5_L
    p = S5_PAIRS
    wcat, ws2o, alr, ali = s5_tables(a_re, a_im, b_re, b_im, c_re, c_im, log_step)
    up = u.reshape(nj, S5_L, p, 2 * S5_CH).transpose(2, 0, 1, 3).reshape(p, nj, _S5_PW)
    dt = jnp.tile(d_skip.astype(F32).reshape(p, 1, 2 * S5_CH), (1, 1, S5_L))
    pair3 = lambda shape: pl.BlockSpec((1,) + shape, lambda i: (i, 0, 0))
    lanes = lambda: pl.BlockSpec((nj, _S5_SW), lambda i: (0, i))
    yi, vre, vim = pl.pallas_call(
        _s5_in_kernel,
        grid=(p,),
        in_specs=[pair3((nj, _S5_PW)), pair3((_S5_PW, _S5_PW + 2 * _S5_SW)), pair3((1, _S5_PW))],
        out_specs=[pair3((nj, _S5_PW)), lanes(), lanes()],
        out_shape=[jax.ShapeDtypeStruct((p, nj, _S5_PW), F32),
                   jax.ShapeDtypeStruct((nj, p * _S5_SW), F32),
                   jax.ShapeDtypeStruct((nj, p * _S5_SW), F32)],
        compiler_params=_cparams(("parallel",)),
        name="s5_in",
    )(up, wcat, dt)
    sw = p * _S5_SW
    tile = (8, sw // 8)
    lb = tile[1] // 2
    sblk = lambda: pl.BlockSpec((nj, 8, lb), lambda i: (0, 0, i))
    ablk = lambda: pl.BlockSpec((8, lb), lambda i: (0, i))
    xre, xim = pl.pallas_call(
        functools.partial(_s5_scan_kernel, nsteps=nj),
        grid=(2,),
        in_specs=[sblk(), sblk(), ablk(), ablk()],
        out_specs=[sblk(), sblk()],
        out_shape=[jax.ShapeDtypeStruct((nj,) + tile, F32)] * 2,
        compiler_params=_cparams(("parallel",)),
        name="s5_scan",
    )(vre.reshape((nj,) + tile), vim.reshape((nj,) + tile), alr.reshape(tile), ali.reshape(tile))
    yp = pl.pallas_call(
        _s5_out_kernel,
        grid=(p,),
        in_specs=[pair3((nj, _S5_PW)), lanes(), lanes(), pair3((2 * _S5_SW, _S5_PW))],
        out_specs=pair3((nj, _S5_PW)),
        out_shape=jax.ShapeDtypeStruct((p, nj, _S5_PW), F32),
        compiler_params=_cparams(("parallel",)),
        name="s5_out",
    )(yi, xre.reshape(nj, sw), xim.reshape(nj, sw), ws2o)
    y = yp.reshape(p, nj, S5_L, 2 * S5_CH).transpose(1, 2, 0, 3).reshape(t, S5_W)
    tm = min(t, 1024)
    return pl.pallas_call(
        _glu_kernel,
        grid=(t // tm,),
        in_specs=[pl.BlockSpec((tm, S5_W), lambda i: (i, 0)),
                  pl.BlockSpec((S5_W, S5_W), lambda i: (0, 0)),
                  pl.BlockSpec((1, S5_W), lambda i: (0, 0))],
        out_specs=pl.BlockSpec((tm, S5_W), lambda i: (i, 0)),
        out_shape=jax.ShapeDtypeStruct((t, S5_W), BF16),
        compiler_params=_cparams(("parallel",)),
        name="s5_glu",
    )(y, w_glu.astype(BF16), b_glu.astype(F32).reshape(1, S5_W))


def _lower_bounds_kernel(x_ref, o_ref):
    x = x_ref[...]
    e = jnp.exp(x - jnp.max(x, axis=0, keepdims=True))
    p = e / jnp.sum(e, axis=0, keepdims=True)
    run = p[0:1]
    rows = [run - p[0:1]]
    for i in range(1, x.shape[0]):
        run = run + p[i:i + 1]
        rows.append(run - p[0:1])
    o_ref[...] = jnp.concatenate(rows, axis=0)


def lower_bounds(logits):
    return pl.pallas_call(
        _lower_bounds_kernel,
        out_shape=jax.ShapeDtypeStruct(logits.shape, F32),
        name="hgrn_lower_bounds",
    )(logits.astype(F32))


def _out_proj_kernel(h_ref, oa_ref, ob_ref, yc_ref, w1_ref, w2_ref, w3_ref, o_ref):
    o_ref[...] = (h_ref[...]
                  + jnp.dot(oa_ref[...], w1_ref[0].astype(BF16), preferred_element_type=F32)
                  + jnp.dot(ob_ref[...], w2_ref[0].astype(BF16), preferred_element_type=F32)
                  + jnp.dot(yc_ref[...], w3_ref[0].astype(BF16), preferred_element_type=F32))


def out_proj(h, oa, ob, yc, w, layer, *, tm, tn):
    m, n = h.shape
    hw = MIX_HEADS_W
    return pl.pallas_call(
        _out_proj_kernel,
        grid=(m // tm, n // tn),
        in_specs=[pl.BlockSpec((tm, tn), lambda i, j: (i, j)),
                  pl.BlockSpec((tm, hw), lambda i, j: (i, 0)),
                  pl.BlockSpec((tm, hw), lambda i, j: (i, 0)),
                  pl.BlockSpec((tm, S5_W), lambda i, j: (i, 0)),
                  pl.BlockSpec((1, hw, tn), lambda i, j: (layer, 0, j)),
                  pl.BlockSpec((1, hw, tn), lambda i, j: (layer, 1, j)),
                  pl.BlockSpec((1, S5_W, tn), lambda i, j: (layer, 2 * hw // S5_W, j))],
        out_specs=pl.BlockSpec((tm, tn), lambda i, j: (i, j)),
        out_shape=jax.ShapeDtypeStruct((m, n), F32),
        compiler_params=_cparams(("parallel", "arbitrary")),
        name="out_proj",
    )(h, oa, ob, yc, w, w, w)


def _xattn_kernel(h_ref, nw_ref, wq_ref, kv_ref, wo_ref, o_ref, wqb_ref, wob_ref):
    @pl.when(pl.program_id(0) == 0)
    def _():
        wqb_ref[...] = wq_ref[0].astype(BF16)
        wob_ref[...] = wo_ref[0].astype(BF16)

    h = h_ref[...]
    q = jnp.dot(_rms(h, nw_ref[...]).astype(BF16), wqb_ref[...], preferred_element_type=F32)
    kv = kv_ref[...]
    outs = []
    for hd in range(XA_HEADS):
        lo, hi = hd * XA_DH, (hd + 1) * XA_DH
        s = _dot_nt(q[:, lo:hi], kv[:, lo:hi]) * (XA_DH ** -0.5)
        e = jnp.exp(s - jnp.max(s, axis=-1, keepdims=True))
        p = e / jnp.sum(e, axis=-1, keepdims=True)
        outs.append(_dot(p, kv[:, XA_W + lo:XA_W + hi]))
    o = jnp.concatenate(outs, axis=1)
    o_ref[...] = h + _dot(o, wob_ref[...])


def xattn(h, nw, wq, kv, wo, layer, *, tm):
    m, d = h.shape
    nw = nw.reshape(1, d)
    return pl.pallas_call(
        _xattn_kernel,
        grid=(m // tm,),
        in_specs=[pl.BlockSpec((tm, d), lambda i: (i, 0)),
                  pl.BlockSpec((1, d), lambda i: (0, 0)),
                  pl.BlockSpec((1, d, XA_W), lambda i: (layer, 0, 0)),
                  pl.BlockSpec(kv.shape, lambda i: (0, 0)),
                  pl.BlockSpec((1, XA_W, d), lambda i: (layer, 0, 0))],
        out_specs=pl.BlockSpec((tm, d), lambda i: (i, 0)),
        out_shape=jax.ShapeDtypeStruct((m, d), F32),
        scratch_shapes=[pltpu.VMEM((d, XA_W), BF16), pltpu.VMEM((XA_W, d), BF16)],
        compiler_params=_cparams(("arbitrary",)),
        name="xattn",
    )(h, nw, wq, kv, wo)


def _ffn_kernel(te_ref, nv_ref, x_ref, nw_ref, wg_ref, wu_ref, wd_ref, o_ref, xn_ref, *, norm, residual, nf, lead):
    i = pl.program_id(0)
    f = pl.program_id(1)
    widx = (0,) * (lead + 1)

    @pl.when(i < nv_ref[0])
    def _():
        @pl.when(f == 0)
        def _():
            x = x_ref[...]
            xn_ref[...] = (_rms(x, nw_ref[...]) if norm else x).astype(BF16)
            o_ref[...] = x if residual else jnp.zeros_like(o_ref)

        xn = xn_ref[...]
        g = jnp.dot(xn, wg_ref[widx].astype(BF16), preferred_element_type=F32)
        u = jnp.dot(xn, wu_ref[widx].astype(BF16), preferred_element_type=F32)
        o_ref[...] += jnp.dot((_silu(g) * u).astype(BF16), wd_ref[widx].astype(BF16), preferred_element_type=F32)

    @pl.when((i >= nv_ref[0]) & (f == nf - 1))
    def _():
        o_ref[...] = jnp.zeros_like(o_ref)


def ffn(x, nw, wg, wu, wd, lead, tile_expert, nvalid, *, tm, tf, norm, residual):
    m, d = x.shape
    dff = wg.shape[-1]
    nf = dff // tf
    lead = tuple(lead)
    ones = (1,) * (len(lead) + 1)

    def row(i, f, te, nv):
        return (jnp.minimum(i, nv[0] - 1), 0)

    def wcol(i, f, te, nv):
        return lead + (te[jnp.minimum(i, nv[0] - 1)], 0, jnp.where(i < nv[0], f, nf - 1))

    def wrow(i, f, te, nv):
        return lead + (te[jnp.minimum(i, nv[0] - 1)], jnp.where(i < nv[0], f, nf - 1), 0)

    return pl.pallas_call(
        functools.partial(_ffn_kernel, norm=norm, residual=residual, nf=nf, lead=len(lead)),
        grid_spec=pltpu.PrefetchScalarGridSpec(
            num_scalar_prefetch=2,
            grid=(m // tm, nf),
            in_specs=[pl.BlockSpec((tm, d), row),
                      pl.BlockSpec((1, d), lambda i, f, te, nv: (0, 0)),
                      pl.BlockSpec(ones + (d, tf), wcol),
                      pl.BlockSpec(ones + (d, tf), wcol),
                      pl.BlockSpec(ones + (tf, d), wrow)],
            out_specs=pl.BlockSpec((tm, d), lambda i, f, te, nv: (i, 0)),
            scratch_shapes=[pltpu.VMEM((tm, d), BF16)]),
        out_shape=jax.ShapeDtypeStruct((m, d), F32),
        compiler_params=_cparams(("arbitrary", "arbitrary")),
        name="ffn",
    )(tile_expert, nvalid, x, nw.reshape(1, d), wg, wu, wd)


def _router_kernel(h_ref, nw_ref, rw_ref, hn_ref, info_ref, cnt_ref, carry_ref, tri_ref, *, tm):
    @pl.when(pl.program_id(0) == 0)
    def _():
        carry_ref[...] = jnp.zeros_like(carry_ref)
        r = lax.broadcasted_iota(jnp.int32, (tm, tm), 0)
        c = lax.broadcasted_iota(jnp.int32, (tm, tm), 1)
        tri_ref[...] = (c < r).astype(BF16)

    hn = _rms(h_ref[...], nw_ref[...])
    hn_ref[...] = hn
    logits = _dot3(hn, rw_ref[...])
    lane = lax.broadcasted_iota(jnp.int32, (tm, LANES), 1).astype(F32)
    neg = -jnp.inf
    lg = jnp.where(lane < N_EXPERTS, logits, neg)
    m1 = jnp.max(lg, axis=1, keepdims=True)
    i1 = jnp.min(jnp.where(lg == m1, lane, float(LANES)), axis=1, keepdims=True)
    lg2 = jnp.where(lane == i1, neg, lg)
    m2 = jnp.max(lg2, axis=1, keepdims=True)
    i2 = jnp.min(jnp.where(lg2 == m2, lane, float(LANES)), axis=1, keepdims=True)
    e2 = jnp.exp(m2 - m1)
    g1 = 1.0 / (1.0 + e2)
    g2 = e2 / (1.0 + e2)
    cnt = jnp.where((lane == i1) | (lane == i2), 1.0, 0.0)
    carry = carry_ref[...]
    before = jnp.dot(tri_ref[...], cnt.astype(BF16), preferred_element_type=F32) + carry
    r1 = jnp.sum(jnp.where(lane == i1, before, 0.0), axis=1, keepdims=True)
    r2 = jnp.sum(jnp.where(lane == i2, before, 0.0), axis=1, keepdims=True)
    carry = carry + jnp.sum(cnt, axis=0, keepdims=True)
    carry_ref[...] = carry
    cnt_ref[...] = jnp.broadcast_to(carry, cnt_ref.shape)
    info = jnp.zeros((tm, LANES), F32)
    for ln, val in enumerate((i1, i2, r1, r2, g1, g2)):
        info = jnp.where(lane == ln, val, info)
    info_ref[...] = info


def router(h, nw, rw, *, tm):
    m, d = h.shape
    rwp = jnp.zeros((d, LANES), F32).at[:, :N_EXPERTS].set(rw.astype(F32))
    return pl.pallas_call(
        functools.partial(_router_kernel, tm=tm),
        grid=(m // tm,),
        in_specs=[pl.BlockSpec((tm, d), lambda i: (i, 0)),
                  pl.BlockSpec((1, d), lambda i: (0, 0)),
                  pl.BlockSpec((d, LANES), lambda i: (0, 0))],
        out_specs=[pl.BlockSpec((tm, d), lambda i: (i, 0)),
                   pl.BlockSpec((tm, LANES), lambda i: (i, 0)),
                   pl.BlockSpec((8, LANES), lambda i: (0, 0))],
        out_shape=[jax.ShapeDtypeStruct((m, d), F32),
                   jax.ShapeDtypeStruct((m, LANES), F32),
                   jax.ShapeDtypeStruct((8, LANES), F32)],
        scratch_shapes=[pltpu.VMEM((1, LANES), F32), pltpu.VMEM((tm, tm), BF16)],
        compiler_params=_cparams(("arbitrary",)),
        name="moe_router",
    )(h, nw.reshape(1, d), rwp)


def _dispatch_kernel(dest_ref, hn_ref, init_ref, xb_ref, sem, *, tm):
    del init_ref
    base = pl.program_id(0) * tm

    def row_copy(t, d):
        return pltpu.make_async_copy(hn_ref.at[pl.ds(t, 1)], xb_ref.at[pl.ds(d, 1)], sem)

    def issue(t, c):
        row_copy(t, dest_ref[2 * (base + t)]).start()
        row_copy(t, dest_ref[2 * (base + t) + 1]).start()
        return c

    def drain(t, c):
        row_copy(0, 0).wait()
        row_copy(0, 0).wait()
        return c

    lax.fori_loop(0, tm, issue, 0)
    lax.fori_loop(0, tm, drain, 0)


def dispatch(hn, dest, rows, *, tm):
    m, d = hn.shape
    return pl.pallas_call(
        functools.partial(_dispatch_kernel, tm=tm),
        grid_spec=pltpu.PrefetchScalarGridSpec(
            num_scalar_prefetch=1,
            grid=(m // tm,),
            in_specs=[pl.BlockSpec((tm, d), lambda i, dst: (i, 0)),
                      pl.BlockSpec(memory_space=pl.ANY)],
            out_specs=pl.BlockSpec(memory_space=pl.ANY),
            scratch_shapes=[pltpu.SemaphoreType.DMA(())]),
        out_shape=jax.ShapeDtypeStruct((rows, d), F32),
        input_output_aliases={2: 0},
        compiler_params=_cparams(("arbitrary",)),
        name="moe_dispatch",
    )(dest, hn, jnp.zeros((rows, d), F32))


def _combine_kernel(dest_ref, h_ref, info_ref, yb_ref, o_ref, buf_ref, sem, *, tm):
    base = pl.program_id(0) * tm

    def row_copy(t, d, slot):
        return pltpu.make_async_copy(yb_ref.at[pl.ds(d, 1)], buf_ref.at[slot, pl.ds(t, 1)], sem)

    def issue(t, c):
        row_copy(t, dest_ref[2 * (base + t)], 0).start()
        row_copy(t, dest_ref[2 * (base + t) + 1], 1).start()
        return c

    def drain(t, c):
        row_copy(0, 0, 0).wait()
        row_copy(0, 0, 1).wait()
        return c

    lax.fori_loop(0, tm, issue, 0)
    lax.fori_loop(0, tm, drain, 0)
    info = info_ref[...]
    o_ref[...] = h_ref[...] + (info[:, 4:5] * buf_ref[0] + info[:, 5:6] * buf_ref[1])


def combine(h, info, yb, dest, *, tm):
    m, d = h.shape
    return pl.pallas_call(
        functools.partial(_combine_kernel, tm=tm),
        grid_spec=pltpu.PrefetchScalarGridSpec(
            num_scalar_prefetch=1,
            grid=(m // tm,),
            in_specs=[pl.BlockSpec((tm, d), lambda i, dst: (i, 0)),
                      pl.BlockSpec((tm, LANES), lambda i, dst: (i, 0)),
                      pl.BlockSpec(memory_space=pl.ANY)],
            out_specs=pl.BlockSpec((tm, d), lambda i, dst: (i, 0)),
            scratch_shapes=[pltpu.VMEM((2, tm, d), F32), pltpu.SemaphoreType.DMA(())]),
        out_shape=jax.ShapeDtypeStruct((m, d), F32),
        compiler_params=_cparams(("arbitrary",)),
        name="moe_combine",
    )(dest, h, info, yb)


def moe_layer(h, nw, rw, wg, wu, wd, lead):
    m, d = h.shape
    tm = MOE_TM
    ntiles = m * 2 // tm + N_EXPERTS
    hn, info, cnt = router(h, nw, rw, tm=tm)
    counts = cnt[0, :N_EXPERTS].astype(jnp.int32)
    seg = (counts + tm - 1) // tm * tm
    seg_end = jnp.cumsum(seg)
    seg_start = seg_end - seg
    ids = info[:, 0:2].astype(jnp.int32)
    dest = (seg_start[ids] + info[:, 2:4].astype(jnp.int32)).reshape(2 * m)
    tile_start = jnp.arange(ntiles, dtype=jnp.int32) * tm
    tile_expert = jnp.minimum(
        jnp.sum((seg_end[None, :] <= tile_start[:, None]).astype(jnp.int32), axis=1), N_EXPERTS - 1)
    nvalid = (seg_end[-1:] // tm).astype(jnp.int32)
    xb = dispatch(hn, dest, ntiles * tm, tm=256)
    yb = ffn(xb, nw, wg, wu, wd, lead, tile_expert, nvalid, tm=tm, tf=MOE_TF, norm=False, residual=False)
    return combine(h, info, yb, dest, tm=256)


def _final_norm_kernel(x_ref, w_ref, o_ref):
    o_ref[...] = _rms(x_ref[...], w_ref[...])


def final_norm(h, w, *, tm):
    m, d = h.shape
    return pl.pallas_call(
        _final_norm_kernel,
        grid=(m // tm,),
        in_specs=[pl.BlockSpec((tm, d), lambda i: (i, 0)), pl.BlockSpec((1, d), lambda i: (0, 0))],
        out_specs=pl.BlockSpec((tm, d), lambda i: (i, 0)),
        out_shape=jax.ShapeDtypeStruct((m, d), F32),
        compiler_params=_cparams(("parallel",)),
        name="final_norm",
    )(h, w.reshape(1, d))


def _pack_w_in_tail(w_in):
    gates = w_in[:, :, IN_MAIN:IN_MAIN + 2 * N_HEADS]
    u = w_in[:, :, IN_MAIN + 2 * N_HEADS:]
    pad = jnp.zeros(w_in.shape[:2] + (IN_TAIL - S5_W - 2 * N_HEADS,), w_in.dtype)
    return jnp.concatenate([u, gates, pad], axis=2).astype(BF16)


def kernel(x, mem, norm_mix, w_in, w_out, hgrn_lb_logits, hgrn_norm, gdn_conv, gdn_a_log, gdn_dt_bias, gdn_norm, s5_a_re, s5_a_im, s5_b_re, s5_b_im, s5_c_re, s5_c_im, s5_d, s5_log_step, s5_w_glu, s5_b_glu, norm_cross, norm_mem, xa_wq, xa_wk, xa_wv, xa_wo, norm_ffn, ffn_w_gate, ffn_w_up, ffn_w_down, moe_router, moe_w_gate, moe_w_up, moe_w_down, norm_final):
    bsz, t, d = x.shape
    assert bsz == 1
    depth = w_in.shape[0]
    h = x[0].astype(F32)
    memx = mem[0].astype(F32)
    lbs = lower_bounds(hgrn_lb_logits)
    tb = min(t, 512)
    tmr = min(t, 1024)
    dense_tm = min(t, DENSE_TM)
    dense_tiles = t // dense_tm
    w_tail = _pack_w_in_tail(w_in)
    w_kv = jnp.concatenate([xa_wk, xa_wv], axis=2)
    for layer in range(depth):
        proj = norm_matmul(h, norm_mix[layer], w_in, layer, IN_MAIN, tm=tmr, tn=512)
        tail = norm_matmul(h, norm_mix[layer], w_tail, layer, IN_TAIL, tm=tmr, tn=IN_TAIL)
        oa = hgrn_mixer(proj, lbs[layer], hgrn_norm[layer], tb=tb, hpb=HEADS_PER_BODY)
        ob = gdn_mixer(proj, tail, S5_W // LANES, gdn_conv[layer].astype(F32), gdn_a_log[layer],
                       gdn_dt_bias[layer], gdn_norm[layer], tb=tb, hpb=HEADS_PER_BODY)
        yc = s5_mixer(tail[:, :S5_W], s5_a_re[layer], s5_a_im[layer], s5_b_re[layer],
                      s5_b_im[layer], s5_c_re[layer], s5_c_im[layer], s5_d[layer], s5_log_step[layer],
                      s5_w_glu[layer], s5_b_glu[layer])
        h = out_proj(h, oa, ob, yc, w_out, layer, tm=tmr, tn=512)
        kv = norm_matmul(memx, norm_mem[layer], w_kv, layer, 2 * XA_W, tm=memx.shape[0], tn=512, out_dtype=BF16)
        h = xattn(h, norm_cross[layer], xa_wq, kv, xa_wo, layer, tm=MOE_TM)
        i = layer // 2
        if layer % 2 == 0:
            h = ffn(h, norm_ffn[layer], ffn_w_gate, ffn_w_up, ffn_w_down, (), jnp.full((dense_tiles,), i, jnp.int32),
                    jnp.full((1,), dense_tiles, jnp.int32), tm=dense_tm, tf=DENSE_TF, norm=True, residual=True)
        else:
            h = moe_layer(h, norm_ffn[layer], moe_router[i], moe_w_gate, moe_w_up, moe_w_down, (i,))
    return final_norm(h, norm_final, tm=tmr)[None].astype(x.dtype)
```

```python
import functools
import math

import jax
import jax.numpy as jnp
from jax import lax
from jax.experimental import pallas as pl
from jax.experimental.pallas import tpu as pltpu

F32 = jnp.float32
BF16 = jnp.bfloat16

D_MODEL = 2048
N_HEADS = 6
HEAD_DIM = 128
CHUNK = 64
SUB = 16
CONV_K = 4
S5_GROUPS = 32
S5_CH = 16
S5_STATE = 64
S5_W = S5_GROUPS * S5_CH
S5_L = 16
S5_PAIRS = S5_GROUPS // 2
XA_HEADS = 4
XA_DH = 128
XA_W = XA_HEADS * XA_DH
D_FF = 5632
N_EXPERTS = 8
NORM_EPS = 1e-6
LB_FLOOR = 1e-30
LANES = 128
MIX_HEADS_W = N_HEADS * HEAD_DIM
IN_MAIN = 8 * MIX_HEADS_W
IN_TAIL = S5_W + LANES
HEADS_PER_BODY = 6
ROW_TM = 512
MOE_TM = 768
MOE_TF = 512
DENSE_TM = 1024
DENSE_TF = 256
VMEM_LIMIT = 56 * 1024 * 1024


def _cparams(sem, vmem=VMEM_LIMIT):
    return pltpu.CompilerParams(dimension_semantics=sem, vmem_limit_bytes=vmem)


def _dot(a, b):
    return jnp.dot(a.astype(BF16), b.astype(BF16), preferred_element_type=F32)


def _dot_nt(a, b):
    return lax.dot_general(a.astype(BF16), b.astype(BF16), (((1,), (1,)), ((), ())),
                           preferred_element_type=F32)


def _dot_tn(a, b):
    return lax.dot_general(a.astype(BF16), b.astype(BF16), (((0,), (0,)), ((), ())),
                           preferred_element_type=F32)


def _split3(x):
    hi = x.astype(BF16)
    r1 = x - hi.astype(F32)
    mid = r1.astype(BF16)
    lo = (r1 - mid.astype(F32)).astype(BF16)
    return hi, mid, lo


def _dot_exact_lhs(m_bf16, x):
    hi, mid, lo = _split3(x)
    return (jnp.dot(m_bf16, hi, preferred_element_type=F32)
            + jnp.dot(m_bf16, mid, preferred_element_type=F32)
            + jnp.dot(m_bf16, lo, preferred_element_type=F32))


def _rms(x, w):
    ms = jnp.mean(x * x, axis=-1, keepdims=True)
    return x * lax.rsqrt(ms + NORM_EPS) * w


def _silu(x):
    return x * jax.nn.sigmoid(x)


def _norm_matmul_kernel(x_ref, nw_ref, w_ref, o_ref, xn_ref, *, col0, n, ncols):
    @pl.when(pl.program_id(1) == 0)
    def _():
        xn_ref[...] = _rms(x_ref[...], nw_ref[...]).astype(BF16)

    w = w_ref[0]
    tn = w.shape[1]
    if col0 + n > ncols:
        col = col0 + pl.program_id(1) * tn + lax.broadcasted_iota(jnp.int32, (1, tn), 1)
        w = jnp.where(col < ncols, w, 0.0)
    o_ref[...] = jnp.dot(xn_ref[...], w.astype(BF16), preferred_element_type=F32).astype(o_ref.dtype)


def norm_matmul(x, nw, w, layer, col0, n, *, tm, tn, out_dtype=F32):
    m, k = x.shape
    cb0 = col0 // tn
    return pl.pallas_call(
        functools.partial(_norm_matmul_kernel, col0=col0, n=n, ncols=w.shape[2]),
        grid=(m // tm, n // tn),
        in_specs=[pl.BlockSpec((tm, k), lambda i, j: (i, 0)),
                  pl.BlockSpec((1, k), lambda i, j: (0, 0)),
                  pl.BlockSpec((1, k, tn), lambda i, j: (layer, 0, cb0 + j))],
        out_specs=pl.BlockSpec((tm, tn), lambda i, j: (i, j)),
        out_shape=jax.ShapeDtypeStruct((m, n), out_dtype),
        scratch_shapes=[pltpu.VMEM((tm, k), BF16)],
        compiler_params=_cparams(("parallel", "arbitrary")),
        name="norm_matmul",
    )(x, nw.reshape(1, k), w)


def _chunk_masks():
    r = lax.broadcasted_iota(jnp.int32, (CHUNK, CHUNK), 0)
    c = lax.broadcasted_iota(jnp.int32, (CHUNK, CHUNK), 1)
    return r, c


def _round_robin(stages):
    live = list(stages)
    while live:
        nxt = []
        for g in live:
            try:
                next(g)
                nxt.append(g)
            except StopIteration:
                pass
        live = nxt


def _hgrn_kernel(q_ref, f_ref, i_ref, g_ref, lb_ref, nw_ref, o_ref, st_ref, *, nchunk, hpb):
    @pl.when(pl.program_id(1) == 0)
    def _():
        st_ref[...] = jnp.zeros_like(st_ref)

    nsub = CHUNK // SUB
    nw = nw_ref[...]
    r, c = _chunk_masks()
    rb = lax.shift_right_logical(r, 4)
    cb = lax.shift_right_logical(c, 4)
    mats = [(c <= r)] + [(c < SUB * (j + 1)) for j in range(nsub)]
    pmat = jnp.concatenate([m.astype(BF16) for m in mats], axis=0)
    diag_mask = (rb == cb) & (c <= r)
    rcol = lax.broadcasted_iota(jnp.int32, (CHUNK, 1), 0)
    rbcol = lax.shift_right_logical(rcol, 4)

    def head_chunk(sl, hd):
        hs = slice(hd * HEAD_DIM, (hd + 1) * HEAD_DIM)
        lb = lb_ref[:, hs]
        lbf = jnp.maximum(lb, LB_FLOOR)
        oml = 1.0 - lb
        fa = f_ref[sl, hs]
        qa = q_ref[sl, hs]
        v = i_ref[sl, hs]
        ga = g_ref[sl, hs]
        logf = jnp.log(lbf + oml * jax.nn.sigmoid(fa))
        k = oml * jax.nn.sigmoid(-fa)
        q = _silu(qa) * (HEAD_DIM ** -0.5)
        sums = _dot_exact_lhs(pmat, logf)
        yield
        cum = sums[0:CHUNK]
        ends = [sums[CHUNK * (j + 1):CHUNK * (j + 2)] for j in range(nsub)]
        last = ends[nsub - 1]
        base = jnp.zeros_like(cum)
        endv = ends[0]
        for j in range(1, nsub):
            base = jnp.where(rbcol >= j, ends[j - 1], base)
            endv = jnp.where(rbcol >= j, ends[j], endv)
        st = st_ref[hd]
        o = _dot_nt(q * jnp.exp(cum), st)
        st_new = st * jnp.exp(last[0:1, :]) + _dot_tn(v, k * jnp.exp(last - cum))
        attn = jnp.where(diag_mask, _dot_nt(q * jnp.exp(cum - base), k * jnp.exp(base - cum)), 0.0)
        ko = k * jnp.exp(endv - cum)
        for j in range(nsub - 1):
            below = rbcol > j
            qo = jnp.where(below, q * jnp.exp(jnp.where(below, cum - ends[j], 0.0)), 0.0)
            kj = jnp.where(rbcol == j, ko, 0.0)
            attn = attn + _dot_nt(qo, kj)
        yield
        o = o + _dot(attn, v)
        yield
        st_ref[hd] = st_new
        o_ref[sl, hs] = (_rms(o, nw) * _silu(ga)).astype(o_ref.dtype)

    def body(ci, carry):
        sl = pl.ds(pl.multiple_of(ci * CHUNK, CHUNK), CHUNK)
        _round_robin([head_chunk(sl, hd) for hd in range(hpb)])
        return carry

    lax.fori_loop(0, nchunk, body, 0)


def hgrn_mixer(proj, lb, nw, *, tb, hpb):
    t = proj.shape[0]
    nchunk = tb // CHUNK
    ngrp = N_HEADS // hpb
    w = hpb * HEAD_DIM

    def col(part):
        return pl.BlockSpec((tb, w), lambda g, i, part=part: (i, part * ngrp + g))

    return pl.pallas_call(
        functools.partial(_hgrn_kernel, nchunk=nchunk, hpb=hpb),
        grid=(ngrp, t // tb),
        in_specs=[col(0), col(1), col(2), col(3),
                  pl.BlockSpec((1, w), lambda g, i: (0, g)),
                  pl.BlockSpec((1, HEAD_DIM), lambda g, i: (0, 0))],
        out_specs=pl.BlockSpec((tb, w), lambda g, i: (i, g)),
        out_shape=jax.ShapeDtypeStruct((t, MIX_HEADS_W), BF16),
        scratch_shapes=[pltpu.VMEM((hpb, HEAD_DIM, HEAD_DIM), F32)],
        compiler_params=_cparams(("parallel", "arbitrary")),
        name="hgrn_mixer",
    )(proj, proj, proj, proj, lb.reshape(1, MIX_HEADS_W), nw.reshape(1, HEAD_DIM))


def _lane_pick(x, lane_ids, idx):
    return jnp.sum(jnp.where(lane_ids == idx, x, 0.0), axis=1, keepdims=True)


def _gdn_kernel(q_ref, k_ref, v_ref, z_ref, ab_ref, cq_ref, ck_ref, cv_ref, alog_ref, dtb_ref, nw_ref,
                o_ref, st_ref, xq_ref, xk_ref, xv_ref, *, nchunk, tb, hpb):
    head0 = pl.program_id(0) * hpb
    tail = 8
    width = hpb * HEAD_DIM

    @pl.when(pl.program_id(1) == 0)
    def _():
        st_ref[...] = jnp.zeros_like(st_ref)
        for xr in (xq_ref, xk_ref, xv_ref):
            xr[0:tail, :] = jnp.zeros((tail, width), F32)

    for src, xr, cw in ((q_ref, xq_ref, cq_ref), (k_ref, xk_ref, ck_ref), (v_ref, xv_ref, cv_ref)):
        xr[tail:tail + tb, :] = src[...]
        acc = jnp.zeros((tb, width), F32)
        for j in range(CONV_K):
            acc = acc + cw[j:j + 1, :] * xr[pl.ds(tail - (CONV_K - 1) + j, tb), :]
        new_tail = xr[tb:tb + tail, :]
        xr[tail:tail + tb, :] = _silu(acc)
        xr[0:tail, :] = new_tail

    nw = nw_ref[...]
    r, c = _chunk_masks()
    incl = c <= r
    strict = c < r
    eye = c == r
    tri = incl.astype(BF16)
    ones = jnp.ones((CHUNK, CHUNK), BF16)
    lane = lax.broadcasted_iota(jnp.int32, (CHUNK, LANES), 1)
    neg_a = -jnp.exp(alog_ref[...])
    dtb = dtb_ref[...]

    def head_chunk(sl, slx, hd, cum_all, beta_all):
        hs = slice(hd * HEAD_DIM, (hd + 1) * HEAD_DIM)
        qc = xq_ref[slx, hs]
        kc = xk_ref[slx, hs]
        v = xv_ref[slx, hs]
        q = qc * lax.rsqrt(jnp.sum(qc * qc, axis=-1, keepdims=True) + 1e-6) * (HEAD_DIM ** -0.5)
        k = kc * lax.rsqrt(jnp.sum(kc * kc, axis=-1, keepdims=True) + 1e-6)
        cum = _lane_pick(cum_all, lane, head0 + hd)
        beta = _lane_pick(beta_all, lane, head0 + hd + N_HEADS)
        cum_b = jnp.broadcast_to(cum, (CHUNK, CHUNK))
        cum_row = _dot_exact_lhs(ones, jnp.where(eye, cum_b, 0.0))
        kk = _dot_nt(k, k)
        qk = _dot_nt(q, k)
        yield
        gam = jnp.where(incl, jnp.exp(jnp.where(incl, cum_b - cum_row, 0.0)), 0.0)
        last = cum[CHUNK - 1:CHUNK, :]
        ecum = jnp.exp(cum)
        a = jnp.where(strict, kk * gam, 0.0) * beta
        st = st_ref[hd]
        o = _dot(q * ecum, st)
        eyef = eye.astype(F32)
        tinv = eyef - a
        pw = _dot(a, a)
        yield
        for _ in range(int(math.log2(CHUNK)) - 2):
            tinv, pw = tinv + _dot(tinv, pw), _dot(pw, pw)
            yield
        tinv = tinv + _dot(tinv, pw)
        yield
        rhs = jnp.concatenate([v * beta, k * (beta * ecum)], axis=1)
        sol = _dot(tinv, rhs)
        yield
        u = sol[:, :HEAD_DIM]
        w = sol[:, HEAD_DIM:]
        v_new = u - _dot(w, st)
        yield
        o = o + _dot(qk * gam, v_new)
        st_ref[hd] = st * jnp.exp(last) + _dot_tn(k * jnp.exp(last - cum), v_new)
        yield
        o_ref[sl, hs] = (_rms(o, nw) * _silu(z_ref[sl, hs])).astype(o_ref.dtype)

    def body(ci, carry):
        off = pl.multiple_of(ci * CHUNK, CHUNK)
        sl = pl.ds(off, CHUNK)
        slx = pl.ds(off + tail, CHUNK)
        gates = ab_ref[sl, :]
        cum_all = _dot_exact_lhs(tri, neg_a * jax.nn.softplus(gates + dtb))
        beta_all = jax.nn.sigmoid(gates)
        _round_robin([head_chunk(sl, slx, hd, cum_all, beta_all) for hd in range(hpb)])
        return carry

    lax.fori_loop(0, nchunk, body, 0)


def gdn_mixer(proj, gates, gate_blk, conv_w, a_log, dt_bias, nw, *, tb, hpb):
    t = proj.shape[0]
    nchunk = tb // CHUNK
    ngrp = N_HEADS // hpb
    w = hpb * HEAD_DIM

    def col(part):
        return pl.BlockSpec((tb, w), lambda g, i, part=part: (i, part * ngrp + g))

    def cw(part):
        return pl.BlockSpec((CONV_K, w), lambda g, i, part=part: (0, part * ngrp + g))

    def row_pad(vec):
        return jnp.zeros((1, LANES), F32).at[0, :N_HEADS].set(vec.astype(F32))

    return pl.pallas_call(
        functools.partial(_gdn_kernel, nchunk=nchunk, tb=tb, hpb=hpb),
        grid=(ngrp, t // tb),
        in_specs=[col(4), col(5), col(6), col(7),
                  pl.BlockSpec((tb, LANES), lambda g, i: (i, gate_blk)),
                  cw(0), cw(1), cw(2),
                  pl.BlockSpec((1, LANES), lambda g, i: (0, 0)),
                  pl.BlockSpec((1, LANES), lambda g, i: (0, 0)),
                  pl.BlockSpec((1, HEAD_DIM), lambda g, i: (0, 0))],
        out_specs=pl.BlockSpec((tb, w), lambda g, i: (i, g)),
        out_shape=jax.ShapeDtypeStruct((t, MIX_HEADS_W), BF16),
        scratch_shapes=[pltpu.VMEM((hpb, HEAD_DIM, HEAD_DIM), F32)]
        + [pltpu.VMEM((tb + 8, w), F32) for _ in range(3)],
        compiler_params=_cparams(("parallel", "arbitrary")),
        name="gdn_mixer",
    )(proj, proj, proj, proj, gates, conv_w, conv_w, conv_w, row_pad(a_log), row_pad(dt_bias),
      nw.reshape(1, HEAD_DIM))


_S5_PW = S5_L * 2 * S5_CH
_S5_SW = 2 * S5_STATE


def _dot3(a, b):
    ah = a.astype(BF16)
    al = (a - ah.astype(F32)).astype(BF16)
    bh = b.astype(BF16)
    bl = (b - bh.astype(F32)).astype(BF16)
    return (jnp.dot(ah, bh, preferred_element_type=F32) + jnp.dot(ah, bl, preferred_element_type=F32)
            + jnp.dot(al, bh, preferred_element_type=F32))


def _s5_tables_kernel(arc_ref, aic_ref, lsc_ref, arr_ref, air_ref, lsr_ref, cre_ref, cim_ref, bre_ref, bim_ref,
                      wcat_ref, ws2o_ref, alr_ref, ali_ref):
    are = jnp.minimum(arc_ref[0], -1e-4)
    aim = aic_ref[0]
    delta = jnp.exp(lsc_ref[0])
    lane = lax.broadcasted_iota(jnp.int32, (1, _S5_PW), 1)
    tau = lax.shift_right_logical(lane, 5).astype(F32)
    mag = jnp.exp(are * delta * tau)
    ang = aim * delta * tau
    lr = mag * jnp.cos(ang)
    li = mag * jnp.sin(ang)
    cre = cre_ref[0]
    cim = cim_ref[0]
    rr0 = jnp.concatenate([lr * cre - li * cim, -(lr * cim + li * cre)], axis=0)
    m1 = jnp.exp(are * delta)
    l1r = m1 * jnp.cos(aim * delta)
    l1i = m1 * jnp.sin(aim * delta)
    lr1 = lr * l1r - li * l1i
    li1 = lr * l1i + li * l1r
    ws2o_ref[0] = jnp.concatenate([lr1 * cre - li1 * cim, -(lr1 * cim + li1 * cre)], axis=0).astype(BF16)

    ar = jnp.minimum(arr_ref[0], -1e-4)
    ai = air_ref[0]
    dl = jnp.exp(lsr_ref[0])
    mb = jnp.exp(ar * dl)
    lbr = mb * jnp.cos(ai * dl)
    lbi = mb * jnp.sin(ai * dl)
    den = ar * ar + ai * ai
    xr = lbr - 1.0
    cr = (xr * ar + lbi * ai) / den
    ci = (lbi * ar - xr * ai) / den
    bre = bre_ref[0]
    bim = bim_ref[0]
    bbr = cr * bre - ci * bim
    bbi = cr * bim + ci * bre
    g = _dot3(jnp.concatenate([bbr, bbi], axis=1), rr0)
    rows_per = 2 * S5_CH
    for s in range(S5_L):
        if s == 0:
            blk = g
        else:
            blk = jnp.where(lane >= rows_per * s, pltpu.roll(g, rows_per * s, axis=1), 0.0)
        wcat_ref[0, rows_per * s:rows_per * (s + 1), 0:_S5_PW] = blk.astype(BF16)
    row = lax.broadcasted_iota(jnp.int32, (_S5_PW, 1), 0)
    mpow = (S5_L - 1 - lax.shift_right_logical(row, 5)).astype(F32)
    pm = jnp.exp(ar * dl * mpow)
    pr = pm * jnp.cos(ai * dl * mpow)
    pi = pm * jnp.sin(ai * dl * mpow)
    bbr_t = jnp.concatenate([bbr] * S5_L, axis=0)
    bbi_t = jnp.concatenate([bbi] * S5_L, axis=0)
    wcat_ref[0, :, _S5_PW:_S5_PW + _S5_SW] = (pr * bbr_t - pi * bbi_t).astype(BF16)
    wcat_ref[0, :, _S5_PW + _S5_SW:] = (pr * bbi_t + pi * bbr_t).astype(BF16)
    ml = jnp.exp(ar * dl * S5_L)
    alr_ref[0] = ml * jnp.cos(ai * dl * S5_L)
    ali_ref[0] = ml * jnp.sin(ai * dl * S5_L)


def s5_tables(a_re, a_im, b_re, b_im, c_re, c_im, log_step):
    p = S5_PAIRS
    ls = jnp.broadcast_to(log_step.astype(F32)[:, None], (S5_GROUPS, S5_STATE))
    same = (jnp.eye(2, dtype=F32) > 0).reshape(1, 2, 2, 1, 1)

    def ctile(cm):
        x = jnp.where(same, cm.astype(F32).reshape(p, 1, 2, S5_CH, S5_STATE), 0.0)
        x = x.transpose(0, 1, 4, 2, 3)[:, :, :, None]
        return jnp.broadcast_to(x, (p, 2, S5_STATE, S5_L, 2, S5_CH)).reshape(p, _S5_SW, _S5_PW)

    def btile(bm):
        x = jnp.where(same, bm.astype(F32).reshape(p, 2, 1, S5_STATE, S5_CH), 0.0)
        return x.transpose(0, 1, 4, 2, 3).reshape(p, 2 * S5_CH, _S5_SW)

    col = lambda v: v.astype(F32).reshape(p, _S5_SW, 1)
    rowv = lambda v: v.astype(F32).reshape(p, 1, _S5_SW)
    spec = lambda shape: pl.BlockSpec((1,) + shape, lambda i: (i, 0, 0))
    return pl.pallas_call(
        _s5_tables_kernel,
        grid=(p,),
        in_specs=[spec((_S5_SW, 1))] * 3 + [spec((1, _S5_SW))] * 3
        + [spec((_S5_SW, _S5_PW))] * 2 + [spec((2 * S5_CH, _S5_SW))] * 2,
        out_specs=[spec((_S5_PW, _S5_PW + 2 * _S5_SW)), spec((2 * _S5_SW, _S5_PW)),
                   spec((1, _S5_SW)), spec((1, _S5_SW))],
        out_shape=[jax.ShapeDtypeStruct((p, _S5_PW, _S5_PW + 2 * _S5_SW), BF16),
                   jax.ShapeDtypeStruct((p, 2 * _S5_SW, _S5_PW), BF16),
                   jax.ShapeDtypeStruct((p, 1, _S5_SW), F32),
                   jax.ShapeDtypeStruct((p, 1, _S5_SW), F32)],
        compiler_params=_cparams(("parallel",)),
        name="s5_tables",
    )(col(a_re), col(a_im), col(ls), rowv(a_re), rowv(a_im), rowv(ls),
      ctile(c_re), ctile(c_im), btile(b_re), btile(b_im))


def _s5_in_kernel(u_ref, wcat_ref, d_ref, yi_ref, vre_ref, vim_ref):
    u = u_ref[0]
    ycat = jnp.dot(u.astype(BF16), wcat_ref[0], preferred_element_type=F32)
    yi_ref[0] = ycat[:, :_S5_PW] + d_ref[0] * u
    vre_ref[...] = ycat[:, _S5_PW:_S5_PW + _S5_SW]
    vim_ref[...] = ycat[:, _S5_PW + _S5_SW:]


def _s5_scan_kernel(vre_ref, vim_ref, alr_ref, ali_ref, xre_ref, xim_ref, *, nsteps):
    ar = alr_ref[...]
    ai = ali_ref[...]

    def body(j, carry):
        xr, xi = carry
        xre_ref[j] = xr
        xim_ref[j] = xi
        return (ar * xr - ai * xi + vre_ref[j], ar * xi + ai * xr + vim_ref[j])

    zero = jnp.zeros(ar.shape, F32)
    lax.fori_loop(0, nsteps, body, (zero, zero))


def _gelu_tanh(x):
    return 0.5 * x * (1.0 + jnp.tanh(math.sqrt(2.0 / math.pi) * (x + 0.044715 * (x * x * x))))


def _s5_out_kernel(yi_ref, xre_ref, xim_ref, ws2o_ref, y_ref):
    w = ws2o_ref[0]
    y = (yi_ref[0] + jnp.dot(xre_ref[...].astype(BF16), w[:_S5_SW], preferred_element_type=F32)
         + jnp.dot(xim_ref[...].astype(BF16), w[_S5_SW:], preferred_element_type=F32))
    y_ref[0] = _gelu_tanh(y)


def _glu_kernel(y_ref, w_ref, b_ref, o_ref):
    y = y_ref[...]
    z = jnp.dot(y.astype(BF16), w_ref[...], preferred_element_type=F32) + b_ref[...]
    o_ref[...] = (y * jax.nn.sigmoid(z)).astype(o_ref.dtype)


def s5_mixer(u, a_re, a_im, b_re, b_im, c_re, c_im, d_skip, log_step, w_glu, b_glu):
    t = u.shape[0]
    nj = t // S5_L
    p = S5_PAIRS
    wcat, ws2o, alr, ali = s5_tables(a_re, a_im, b_re, b_im, c_re, c_im, log_step)
    up = u.reshape(nj, S5_L, p, 2 * S5_CH).transpose(2, 0, 1, 3).reshape(p, nj, _S5_PW)
    dt = jnp.tile(d_skip.astype(F32).reshape(p, 1, 2 * S5_CH), (1, 1, S5_L))
    pair3 = lambda shape: pl.BlockSpec((1,) + shape, lambda i: (i, 0, 0))
    lanes = lambda: pl.BlockSpec((nj, _S5_SW), lambda i: (0, i))
    yi, vre, vim = pl.pallas_call(
        _s5_in_kernel,
        grid=(p,),
        in_specs=[pair3((nj, _S5_PW)), pair3((_S5_PW, _S5_PW + 2 * _S5_SW)), pair3((1, _S5_PW))],
        out_specs=[pair3((nj, _S5_PW)), lanes(), lanes()],
        out_shape=[jax.ShapeDtypeStruct((p, nj, _S5_PW), F32),
                   jax.ShapeDtypeStruct((nj, p * _S5_SW), F32),
                   jax.ShapeDtypeStruct((nj, p * _S5_SW), F32)],
        compiler_params=_cparams(("parallel",)),
        name="s5_in",
    )(up, wcat, dt)
    sw = p * _S5_SW
    tile = (8, sw // 8)
    lb = tile[1] // 2
    sblk = lambda: pl.BlockSpec((nj, 8, lb), lambda i: (0, 0, i))
    ablk = lambda: pl.BlockSpec((8, lb), lambda i: (0, i))
    xre, xim = pl.pallas_call(
        functools.partial(_s5_scan_kernel, nsteps=nj),
        grid=(2,),
        in_specs=[sblk(), sblk(), ablk(), ablk()],
        out_specs=[sblk(), sblk()],
        out_shape=[jax.ShapeDtypeStruct((nj,) + tile, F32)] * 2,
        compiler_params=_cparams(("parallel",)),
        name="s5_scan",
    )(vre.reshape((nj,) + tile), vim.reshape((nj,) + tile), alr.reshape(tile), ali.reshape(tile))
    yp = pl.pallas_call(
        _s5_out_kernel,
        grid=(p,),
        in_specs=[pair3((nj, _S5_PW)), lanes(), lanes(), pair3((2 * _S5_SW, _S5_PW))],
        out_specs=pair3((nj, _S5_PW)),
        out_shape=jax.ShapeDtypeStruct((p, nj, _S5_PW), F32),
        compiler_params=_cparams(("parallel",)),
        name="s5_out",
    )(yi, xre.reshape(nj, sw), xim.reshape(nj, sw), ws2o)
    y = yp.reshape(p, nj, S5_L, 2 * S5_CH).transpose(1, 2, 0, 3).reshape(t, S5_W)
    tm = min(t, 1024)
    return pl.pallas_call(
        _glu_kernel,
        grid=(t // tm,),
        in_specs=[pl.BlockSpec((tm, S5_W), lambda i: (i, 0)),
                  pl.BlockSpec((S5_W, S5_W), lambda i: (0, 0)),
                  pl.BlockSpec((1, S5_W), lambda i: (0, 0))],
        out_specs=pl.BlockSpec((tm, S5_W), lambda i: (i, 0)),
        out_shape=jax.ShapeDtypeStruct((t, S5_W), BF16),
        compiler_params=_cparams(("parallel",)),
        name="s5_glu",
    )(y, w_glu.astype(BF16), b_glu.astype(F32).reshape(1, S5_W))


def _lower_bounds_kernel(x_ref, o_ref):
    x = x_ref[...]
    e = jnp.exp(x - jnp.max(x, axis=0, keepdims=True))
    p = e / jnp.sum(e, axis=0, keepdims=True)
    run = p[0:1]
    rows = [run - p[0:1]]
    for i in range(1, x.shape[0]):
        run = run + p[i:i + 1]
        rows.append(run - p[0:1])
    o_ref[...] = jnp.concatenate(rows, axis=0)


def lower_bounds(logits):
    return pl.pallas_call(
        _lower_bounds_kernel,
        out_shape=jax.ShapeDtypeStruct(logits.shape, F32),
        name="hgrn_lower_bounds",
    )(logits.astype(F32))


def _out_proj_kernel(h_ref, oa_ref, ob_ref, yc_ref, w1_ref, w2_ref, w3_ref, o_ref):
    o_ref[...] = (h_ref[...]
                  + jnp.dot(oa_ref[...], w1_ref[0].astype(BF16), preferred_element_type=F32)
                  + jnp.dot(ob_ref[...], w2_ref[0].astype(BF16), preferred_element_type=F32)
                  + jnp.dot(yc_ref[...], w3_ref[0].astype(BF16), preferred_element_type=F32))


def out_proj(h, oa, ob, yc, w, layer, *, tm, tn):
    m, n = h.shape
    hw = MIX_HEADS_W
    return pl.pallas_call(
        _out_proj_kernel,
        grid=(m // tm, n // tn),
        in_specs=[pl.BlockSpec((tm, tn), lambda i, j: (i, j)),
                  pl.BlockSpec((tm, hw), lambda i, j: (i, 0)),
                  pl.BlockSpec((tm, hw), lambda i, j: (i, 0)),
                  pl.BlockSpec((tm, S5_W), lambda i, j: (i, 0)),
                  pl.BlockSpec((1, hw, tn), lambda i, j: (layer, 0, j)),
                  pl.BlockSpec((1, hw, tn), lambda i, j: (layer, 1, j)),
                  pl.BlockSpec((1, S5_W, tn), lambda i, j: (layer, 2 * hw // S5_W, j))],
        out_specs=pl.BlockSpec((tm, tn), lambda i, j: (i, j)),
        out_shape=jax.ShapeDtypeStruct((m, n), F32),
        compiler_params=_cparams(("parallel", "arbitrary")),
        name="out_proj",
    )(h, oa, ob, yc, w, w, w)


def _xattn_kernel(h_ref, nw_ref, wq_ref, kv_ref, wo_ref, o_ref, wqb_ref, wob_ref):
    @pl.when(pl.program_id(0) == 0)
    def _():
        wqb_ref[...] = wq_ref[0].astype(BF16)
        wob_ref[...] = wo_ref[0].astype(BF16)

    h = h_ref[...]
    q = jnp.dot(_rms(h, nw_ref[...]).astype(BF16), wqb_ref[...], preferred_element_type=F32)
    kv = kv_ref[...]
    outs = []
    for hd in range(XA_HEADS):
        lo, hi = hd * XA_DH, (hd + 1) * XA_DH
        s = _dot_nt(q[:, lo:hi], kv[:, lo:hi]) * (XA_DH ** -0.5)
        e = jnp.exp(s - jnp.max(s, axis=-1, keepdims=True))
        p = e / jnp.sum(e, axis=-1, keepdims=True)
        outs.append(_dot(p, kv[:, XA_W + lo:XA_W + hi]))
    o = jnp.concatenate(outs, axis=1)
    o_ref[...] = h + _dot(o, wob_ref[...])


def xattn(h, nw, wq, kv, wo, layer, *, tm):
    m, d = h.shape
    nw = nw.reshape(1, d)
    return pl.pallas_call(
        _xattn_kernel,
        grid=(m // tm,),
        in_specs=[pl.BlockSpec((tm, d), lambda i: (i, 0)),
                  pl.BlockSpec((1, d), lambda i: (0, 0)),
                  pl.BlockSpec((1, d, XA_W), lambda i: (layer, 0, 0)),
                  pl.BlockSpec(kv.shape, lambda i: (0, 0)),
                  pl.BlockSpec((1, XA_W, d), lambda i: (layer, 0, 0))],
        out_specs=pl.BlockSpec((tm, d), lambda i: (i, 0)),
        out_shape=jax.ShapeDtypeStruct((m, d), F32),
        scratch_shapes=[pltpu.VMEM((d, XA_W), BF16), pltpu.VMEM((XA_W, d), BF16)],
        compiler_params=_cparams(("arbitrary",)),
        name="xattn",
    )(h, nw, wq, kv, wo)


def _ffn_kernel(te_ref, nv_ref, x_ref, nw_ref, wg_ref, wu_ref, wd_ref, o_ref, xn_ref, *, norm, residual, nf, lead):
    i = pl.program_id(0)
    f = pl.program_id(1)
    widx = (0,) * (lead + 1)

    @pl.when(i < nv_ref[0])
    def _():
        @pl.when(f == 0)
        def _():
            x = x_ref[...]
            xn_ref[...] = (_rms(x, nw_ref[...]) if norm else x).astype(BF16)
            o_ref[...] = x if residual else jnp.zeros_like(o_ref)

        xn = xn_ref[...]
        g = jnp.dot(xn, wg_ref[widx].astype(BF16), preferred_element_type=F32)
        u = jnp.dot(xn, wu_ref[widx].astype(BF16), preferred_element_type=F32)
        o_ref[...] += jnp.dot((_silu(g) * u).astype(BF16), wd_ref[widx].astype(BF16), preferred_element_type=F32)

    @pl.when((i >= nv_ref[0]) & (f == nf - 1))
    def _():
        o_ref[...] = jnp.zeros_like(o_ref)


def ffn(x, nw, wg, wu, wd, lead, tile_expert, nvalid, *, tm, tf, norm, residual):
    m, d = x.shape
    dff = wg.shape[-1]
    nf = dff // tf
    lead = tuple(lead)
    ones = (1,) * (len(lead) + 1)

    def row(i, f, te, nv):
        return (jnp.minimum(i, nv[0] - 1), 0)

    def wcol(i, f, te, nv):
        return lead + (te[jnp.minimum(i, nv[0] - 1)], 0, jnp.where(i < nv[0], f, nf - 1))

    def wrow(i, f, te, nv):
        return lead + (te[jnp.minimum(i, nv[0] - 1)], jnp.where(i < nv[0], f, nf - 1), 0)

    return pl.pallas_call(
        functools.partial(_ffn_kernel, norm=norm, residual=residual, nf=nf, lead=len(lead)),
        grid_spec=pltpu.PrefetchScalarGridSpec(
            num_scalar_prefetch=2,
            grid=(m // tm, nf),
            in_specs=[pl.BlockSpec((tm, d), row),
                      pl.BlockSpec((1, d), lambda i, f, te, nv: (0, 0)),
                      pl.BlockSpec(ones + (d, tf), wcol),
                      pl.BlockSpec(ones + (d, tf), wcol),
                      pl.BlockSpec(ones + (tf, d), wrow)],
            out_specs=pl.BlockSpec((tm, d), lambda i, f, te, nv: (i, 0)),
            scratch_shapes=[pltpu.VMEM((tm, d), BF16)]),
        out_shape=jax.ShapeDtypeStruct((m, d), F32),
        compiler_params=_cparams(("arbitrary", "arbitrary")),
        name="ffn",
    )(tile_expert, nvalid, x, nw.reshape(1, d), wg, wu, wd)


def _router_kernel(h_ref, nw_ref, rw_ref, hn_ref, info_ref, cnt_ref, carry_ref, tri_ref, *, tm):
    @pl.when(pl.program_id(0) == 0)
    def _():
        carry_ref[...] = jnp.zeros_like(carry_ref)
        r = lax.broadcasted_iota(jnp.int32, (tm, tm), 0)
        c = lax.broadcasted_iota(jnp.int32, (tm, tm), 1)
        tri_ref[...] = (c < r).astype(BF16)

    hn = _rms(h_ref[...], nw_ref[...])
    hn_ref[...] = hn
    logits = _dot3(hn, rw_ref[...])
    lane = lax.broadcasted_iota(jnp.int32, (tm, LANES), 1).astype(F32)
    neg = -jnp.inf
    lg = jnp.where(lane < N_EXPERTS, logits, neg)
    m1 = jnp.max(lg, axis=1, keepdims=True)
    i1 = jnp.min(jnp.where(lg == m1, lane, float(LANES)), axis=1, keepdims=True)
    lg2 = jnp.where(lane == i1, neg, lg)
    m2 = jnp.max(lg2, axis=1, keepdims=True)
    i2 = jnp.min(jnp.where(lg2 == m2, lane, float(LANES)), axis=1, keepdims=True)
    e2 = jnp.exp(m2 - m1)
    g1 = 1.0 / (1.0 + e2)
    g2 = e2 / (1.0 + e2)
    cnt = jnp.where((lane == i1) | (lane == i2), 1.0, 0.0)
    carry = carry_ref[...]
    before = jnp.dot(tri_ref[...], cnt.astype(BF16), preferred_element_type=F32) + carry
    r1 = jnp.sum(jnp.where(lane == i1, before, 0.0), axis=1, keepdims=True)
    r2 = jnp.sum(jnp.where(lane == i2, before, 0.0), axis=1, keepdims=True)
    carry = carry + jnp.sum(cnt, axis=0, keepdims=True)
    carry_ref[...] = carry
    cnt_ref[...] = jnp.broadcast_to(carry, cnt_ref.shape)
    info = jnp.zeros((tm, LANES), F32)
    for ln, val in enumerate((i1, i2, r1, r2, g1, g2)):
        info = jnp.where(lane == ln, val, info)
    info_ref[...] = info


def router(h, nw, rw, *, tm):
    m, d = h.shape
    rwp = jnp.zeros((d, LANES), F32).at[:, :N_EXPERTS].set(rw.astype(F32))
    return pl.pallas_call(
        functools.partial(_router_kernel, tm=tm),
        grid=(m // tm,),
        in_specs=[pl.BlockSpec((tm, d), lambda i: (i, 0)),
                  pl.BlockSpec((1, d), lambda i: (0, 0)),
                  pl.BlockSpec((d, LANES), lambda i: (0, 0))],
        out_specs=[pl.BlockSpec((tm, d), lambda i: (i, 0)),
                   pl.BlockSpec((tm, LANES), lambda i: (i, 0)),
                   pl.BlockSpec((8, LANES), lambda i: (0, 0))],
        out_shape=[jax.ShapeDtypeStruct((m, d), F32),
                   jax.ShapeDtypeStruct((m, LANES), F32),
                   jax.ShapeDtypeStruct((8, LANES), F32)],
        scratch_shapes=[pltpu.VMEM((1, LANES), F32), pltpu.VMEM((tm, tm), BF16)],
        compiler_params=_cparams(("arbitrary",)),
        name="moe_router",
    )(h, nw.reshape(1, d), rwp)


def _dispatch_kernel(dest_ref, hn_ref, init_ref, xb_ref, sem, *, tm):
    del init_ref
    base = pl.program_id(0) * tm

    def row_copy(t, d):
        return pltpu.make_async_copy(hn_ref.at[pl.ds(t, 1)], xb_ref.at[pl.ds(d, 1)], sem)

    def issue(t, c):
        row_copy(t, dest_ref[2 * (base + t)]).start()
        row_copy(t, dest_ref[2 * (base + t) + 1]).start()
        return c

    def drain(t, c):
        row_copy(0, 0).wait()
        row_copy(0, 0).wait()
        return c

    lax.fori_loop(0, tm, issue, 0)
    lax.fori_loop(0, tm, drain, 0)


def dispatch(hn, dest, rows, *, tm):
    m, d = hn.shape
    return pl.pallas_call(
        functools.partial(_dispatch_kernel, tm=tm),
        grid_spec=pltpu.PrefetchScalarGridSpec(
            num_scalar_prefetch=1,
            grid=(m // tm,),
            in_specs=[pl.BlockSpec((tm, d), lambda i, dst: (i, 0)),
                      pl.BlockSpec(memory_space=pl.ANY)],
            out_specs=pl.BlockSpec(memory_space=pl.ANY),
            scratch_shapes=[pltpu.SemaphoreType.DMA(())]),
        out_shape=jax.ShapeDtypeStruct((rows, d), F32),
        input_output_aliases={2: 0},
        compiler_params=_cparams(("arbitrary",)),
        name="moe_dispatch",
    )(dest, hn, jnp.zeros((rows, d), F32))


def _combine_kernel(dest_ref, h_ref, info_ref, yb_ref, o_ref, buf_ref, sem, *, tm):
    base = pl.program_id(0) * tm

    def row_copy(t, d, slot):
        return pltpu.make_async_copy(yb_ref.at[pl.ds(d, 1)], buf_ref.at[slot, pl.ds(t, 1)], sem)

    def issue(t, c):
        row_copy(t, dest_ref[2 * (base + t)], 0).start()
        row_copy(t, dest_ref[2 * (base + t) + 1], 1).start()
        return c

    def drain(t, c):
        row_copy(0, 0, 0).wait()
        row_copy(0, 0, 1).wait()
        return c

    lax.fori_loop(0, tm, issue, 0)
    lax.fori_loop(0, tm, drain, 0)
    info = info_ref[...]
    o_ref[...] = h_ref[...] + (info[:, 4:5] * buf_ref[0] + info[:, 5:6] * buf_ref[1])


def combine(h, info, yb, dest, *, tm):
    m, d = h.shape
    return pl.pallas_call(
        functools.partial(_combine_kernel, tm=tm),
        grid_spec=pltpu.PrefetchScalarGridSpec(
            num_scalar_prefetch=1,
            grid=(m // tm,),
            in_specs=[pl.BlockSpec((tm, d), lambda i, dst: (i, 0)),
                      pl.BlockSpec((tm, LANES), lambda i, dst: (i, 0)),
                      pl.BlockSpec(memory_space=pl.ANY)],
            out_specs=pl.BlockSpec((tm, d), lambda i, dst: (i, 0)),
            scratch_shapes=[pltpu.VMEM((2, tm, d), F32), pltpu.SemaphoreType.DMA(())]),
        out_shape=jax.ShapeDtypeStruct((m, d), F32),
        compiler_params=_cparams(("arbitrary",)),
        name="moe_combine",
    )(dest, h, info, yb)


def moe_layer(h, nw, rw, wg, wu, wd, lead):
    m, d = h.shape
    tm = MOE_TM
    ntiles = m * 2 // tm + N_EXPERTS
    hn, info, cnt = router(h, nw, rw, tm=min(m, ROW_TM))
    counts = cnt[0, :N_EXPERTS].astype(jnp.int32)
    seg = (counts + tm - 1) // tm * tm
    seg_end = jnp.cumsum(seg)
    seg_start = seg_end - seg
    ids = info[:, 0:2].astype(jnp.int32)
    dest = (seg_start[ids] + info[:, 2:4].astype(jnp.int32)).reshape(2 * m)
    tile_start = jnp.arange(ntiles, dtype=jnp.int32) * tm
    tile_expert = jnp.minimum(
        jnp.sum((seg_end[None, :] <= tile_start[:, None]).astype(jnp.int32), axis=1), N_EXPERTS - 1)
    nvalid = (seg_end[-1:] // tm).astype(jnp.int32)
    xb = dispatch(hn, dest, ntiles * tm, tm=256)
    yb = ffn(xb, nw, wg, wu, wd, lead, tile_expert, nvalid, tm=tm, tf=MOE_TF, norm=False, residual=False)
    return combine(h, info, yb, dest, tm=256)


def _final_norm_kernel(x_ref, w_ref, o_ref):
    o_ref[...] = _rms(x_ref[...], w_ref[...])


def final_norm(h, w, *, tm):
    m, d = h.shape
    return pl.pallas_call(
        _final_norm_kernel,
        grid=(m // tm,),
        in_specs=[pl.BlockSpec((tm, d), lambda i: (i, 0)), pl.BlockSpec((1, d), lambda i: (0, 0))],
        out_specs=pl.BlockSpec((tm, d), lambda i: (i, 0)),
        out_shape=jax.ShapeDtypeStruct((m, d), F32),
        compiler_params=_cparams(("parallel",)),
        name="final_norm",
    )(h, w.reshape(1, d))


def kernel(x, mem, norm_mix, w_in, w_out, hgrn_lb_logits, hgrn_norm, gdn_conv, gdn_a_log, gdn_dt_bias, gdn_norm, s5_a_re, s5_a_im, s5_b_re, s5_b_im, s5_c_re, s5_c_im, s5_d, s5_log_step, s5_w_glu, s5_b_glu, norm_cross, norm_mem, xa_wq, xa_wk, xa_wv, xa_wo, norm_ffn, ffn_w_gate, ffn_w_up, ffn_w_down, moe_router, moe_w_gate, moe_w_up, moe_w_down, norm_final):
    bsz, t, d = x.shape
    assert bsz == 1
    depth = w_in.shape[0]
    h = x[0].astype(F32)
    memx = mem[0].astype(F32)
    lbs = lower_bounds(hgrn_lb_logits)
    tb = min(t, 512)
    tmr = min(t, 1024)
    dense_tm = min(t, DENSE_TM)
    dense_tiles = t // dense_tm
    w_kv = jnp.concatenate([xa_wk, xa_wv], axis=2)
    ngate = 2 * N_HEADS
    for layer in range(depth):
        proj = norm_matmul(h, norm_mix[layer], w_in, layer, 0, IN_MAIN, tm=tmr, tn=512)
        tail = norm_matmul(h, norm_mix[layer], w_in, layer, IN_MAIN, IN_TAIL, tm=tmr, tn=LANES)
        oa = hgrn_mixer(proj, lbs[layer], hgrn_norm[layer], tb=tb, hpb=HEADS_PER_BODY)
        ob = gdn_mixer(proj, tail, 0, gdn_conv[layer].astype(F32), gdn_a_log[layer],
                       gdn_dt_bias[layer], gdn_norm[layer], tb=tb, hpb=HEADS_PER_BODY)
        yc = s5_mixer(tail[:, ngate:ngate + S5_W], s5_a_re[layer], s5_a_im[layer], s5_b_re[layer],
                      s5_b_im[layer], s5_c_re[layer], s5_c_im[layer], s5_d[layer], s5_log_step[layer],
                      s5_w_glu[layer], s5_b_glu[layer])
        h = out_proj(h, oa, ob, yc, w_out, layer, tm=tmr, tn=512)
        kv = norm_matmul(memx, norm_mem[layer], w_kv, layer, 0, 2 * XA_W, tm=memx.shape[0], tn=512,
                         out_dtype=BF16)
        h = xattn(h, norm_cross[layer], xa_wq, kv, xa_wo, layer, tm=min(t, ROW_TM))
        i = layer // 2
        if layer % 2 == 0:
            h = ffn(h, norm_ffn[layer], ffn_w_gate, ffn_w_up, ffn_w_down, (), jnp.full((dense_tiles,), i, jnp.int32),
                    jnp.full((1,), dense_tiles, jnp.int32), tm=dense_tm, tf=DENSE_TF, norm=True, residual=True)
        else:
            h = moe_layer(h, norm_ffn[layer], moe_router[i], moe_w_gate, moe_w_up, moe_w_down, (i,))
    return final_norm(h, norm_final, tm=tmr)[None].astype(x.dtype)
```

```python
import functools
import math

import jax
import jax.numpy as jnp
from jax import lax
from jax.experimental import pallas as pl
from jax.experimental.pallas import tpu as pltpu

F32 = jnp.float32
BF16 = jnp.bfloat16

D_MODEL = 2048
N_HEADS = 6
HEAD_DIM = 128
CHUNK = 64
SUB = 16
CONV_K = 4
S5_GROUPS = 32
S5_CH = 16
S5_STATE = 64
S5_W = S5_GROUPS * S5_CH
S5_L = 16
S5_PAIRS = S5_GROUPS // 2
XA_HEADS = 4
XA_DH = 128
XA_W = XA_HEADS * XA_DH
D_FF = 5632
N_EXPERTS = 8
NORM_EPS = 1e-6
LB_FLOOR = 1e-30
LANES = 128
MIX_HEADS_W = N_HEADS * HEAD_DIM
IN_MAIN = 8 * MIX_HEADS_W
IN_WIDE = IN_MAIN + S5_W
HEADS_PER_BODY = 6
ROW_TM = 512
MOE_TM = 768
MOE_TF = 512
DENSE_TM = 1024
DENSE_TF = 256
VMEM_LIMIT = 56 * 1024 * 1024


def _cparams(sem, vmem=VMEM_LIMIT):
    return pltpu.CompilerParams(dimension_semantics=sem, vmem_limit_bytes=vmem)


def _dot(a, b):
    return jnp.dot(a.astype(BF16), b.astype(BF16), preferred_element_type=F32)


def _dot_nt(a, b):
    return lax.dot_general(a.astype(BF16), b.astype(BF16), (((1,), (1,)), ((), ())),
                           preferred_element_type=F32)


def _dot_tn(a, b):
    return lax.dot_general(a.astype(BF16), b.astype(BF16), (((0,), (0,)), ((), ())),
                           preferred_element_type=F32)


def _split3(x):
    hi = x.astype(BF16)
    r1 = x - hi.astype(F32)
    mid = r1.astype(BF16)
    lo = (r1 - mid.astype(F32)).astype(BF16)
    return hi, mid, lo


def _dot_exact_lhs(m_bf16, x):
    hi, mid, lo = _split3(x)
    return (jnp.dot(m_bf16, hi, preferred_element_type=F32)
            + jnp.dot(m_bf16, mid, preferred_element_type=F32)
            + jnp.dot(m_bf16, lo, preferred_element_type=F32))


def _rms(x, w):
    ms = jnp.mean(x * x, axis=-1, keepdims=True)
    return x * lax.rsqrt(ms + NORM_EPS) * w


def _silu(x):
    return x * jax.nn.sigmoid(x)


def _norm_matmul_kernel(x_ref, nw_ref, w_ref, o_ref, xn_ref, *, col0, n, ncols):
    @pl.when(pl.program_id(1) == 0)
    def _():
        xn_ref[...] = _rms(x_ref[...], nw_ref[...]).astype(BF16)

    w = w_ref[0]
    tn = w.shape[1]
    if col0 + n > ncols:
        col = col0 + pl.program_id(1) * tn + lax.broadcasted_iota(jnp.int32, (1, tn), 1)
        w = jnp.where(col < ncols, w, 0.0)
    o_ref[...] = jnp.dot(xn_ref[...], w.astype(BF16), preferred_element_type=F32).astype(o_ref.dtype)


def norm_matmul(x, nw, w, layer, col0, n, *, tm, tn, out_dtype=F32):
    m, k = x.shape
    cb0 = col0 // tn
    return pl.pallas_call(
        functools.partial(_norm_matmul_kernel, col0=col0, n=n, ncols=w.shape[2]),
        grid=(m // tm, n // tn),
        in_specs=[pl.BlockSpec((tm, k), lambda i, j: (i, 0)),
                  pl.BlockSpec((1, k), lambda i, j: (0, 0)),
                  pl.BlockSpec((1, k, tn), lambda i, j: (layer, 0, cb0 + j))],
        out_specs=pl.BlockSpec((tm, tn), lambda i, j: (i, j)),
        out_shape=jax.ShapeDtypeStruct((m, n), out_dtype),
        scratch_shapes=[pltpu.VMEM((tm, k), BF16)],
        compiler_params=_cparams(("parallel", "arbitrary")),
        name="norm_matmul",
    )(x, nw.reshape(1, k), w)


def _chunk_masks():
    r = lax.broadcasted_iota(jnp.int32, (CHUNK, CHUNK), 0)
    c = lax.broadcasted_iota(jnp.int32, (CHUNK, CHUNK), 1)
    return r, c


def _round_robin(stages):
    live = list(stages)
    while live:
        nxt = []
        for g in live:
            try:
                next(g)
                nxt.append(g)
            except StopIteration:
                pass
        live = nxt


_STATE = "state"


def _run_chunk_pair(first, second):
    live = list(first) + list(second)
    while live:
        live = [g for g in live if next(g) != _STATE]
    _round_robin(first)
    _round_robin(second)


def _hgrn_kernel(q_ref, f_ref, i_ref, g_ref, lb_ref, nw_ref, o_ref, st_ref, *, nchunk, hpb):
    @pl.when(pl.program_id(1) == 0)
    def _():
        st_ref[...] = jnp.zeros_like(st_ref)

    nsub = CHUNK // SUB
    nw = nw_ref[...]
    r, c = _chunk_masks()
    rb = lax.shift_right_logical(r, 4)
    cb = lax.shift_right_logical(c, 4)
    mats = [(c <= r)] + [(c < SUB * (j + 1)) for j in range(nsub)]
    pmat = jnp.concatenate([m.astype(BF16) for m in mats], axis=0)
    diag_mask = (rb == cb) & (c <= r)
    rcol = lax.broadcasted_iota(jnp.int32, (CHUNK, 1), 0)
    rbcol = lax.shift_right_logical(rcol, 4)

    def head_chunk(sl, hd):
        hs = slice(hd * HEAD_DIM, (hd + 1) * HEAD_DIM)
        lb = lb_ref[:, hs]
        lbf = jnp.maximum(lb, LB_FLOOR)
        oml = 1.0 - lb
        fa = f_ref[sl, hs]
        qa = q_ref[sl, hs]
        v = i_ref[sl, hs]
        ga = g_ref[sl, hs]
        logf = jnp.log(lbf + oml * jax.nn.sigmoid(fa))
        k = oml * jax.nn.sigmoid(-fa)
        q = _silu(qa) * (HEAD_DIM ** -0.5)
        sums = _dot_exact_lhs(pmat, logf)
        yield
        cum = sums[0:CHUNK]
        ends = [sums[CHUNK * (j + 1):CHUNK * (j + 2)] for j in range(nsub)]
        last = ends[nsub - 1]
        base = jnp.zeros_like(cum)
        endv = ends[0]
        for j in range(1, nsub):
            base = jnp.where(rbcol >= j, ends[j - 1], base)
            endv = jnp.where(rbcol >= j, ends[j], endv)
        kv = _dot_tn(v, k * jnp.exp(last - cum))
        attn = jnp.where(diag_mask, _dot_nt(q * jnp.exp(cum - base), k * jnp.exp(base - cum)), 0.0)
        ko = k * jnp.exp(endv - cum)
        for j in range(nsub - 1):
            below = rbcol > j
            qo = jnp.where(below, q * jnp.exp(jnp.where(below, cum - ends[j], 0.0)), 0.0)
            kj = jnp.where(rbcol == j, ko, 0.0)
            attn = attn + _dot_nt(qo, kj)
        yield
        o = _dot(attn, v)
        qe = q * jnp.exp(cum)
        yield _STATE
        st = st_ref[hd]
        o = o + _dot_nt(qe, st)
        st_ref[hd] = st * jnp.exp(last[0:1, :]) + kv
        yield
        o_ref[sl, hs] = (_rms(o, nw) * _silu(ga)).astype(o_ref.dtype)

    def body(ci, carry):
        off = pl.multiple_of(ci * (2 * CHUNK), 2 * CHUNK)
        off2 = pl.multiple_of(off + CHUNK, CHUNK)
        _run_chunk_pair([head_chunk(pl.ds(off, CHUNK), hd) for hd in range(hpb)],
                        [head_chunk(pl.ds(off2, CHUNK), hd) for hd in range(hpb)])
        return carry

    lax.fori_loop(0, nchunk // 2, body, 0)


def hgrn_mixer(proj, lb, nw, *, tb, hpb):
    t = proj.shape[0]
    nchunk = tb // CHUNK
    ngrp = N_HEADS // hpb
    w = hpb * HEAD_DIM

    def col(part):
        return pl.BlockSpec((tb, w), lambda g, i, part=part: (i, part * ngrp + g))

    return pl.pallas_call(
        functools.partial(_hgrn_kernel, nchunk=nchunk, hpb=hpb),
        grid=(ngrp, t // tb),
        in_specs=[col(0), col(1), col(2), col(3),
                  pl.BlockSpec((1, w), lambda g, i: (0, g)),
                  pl.BlockSpec((1, HEAD_DIM), lambda g, i: (0, 0))],
        out_specs=pl.BlockSpec((tb, w), lambda g, i: (i, g)),
        out_shape=jax.ShapeDtypeStruct((t, MIX_HEADS_W), BF16),
        scratch_shapes=[pltpu.VMEM((hpb, HEAD_DIM, HEAD_DIM), F32)],
        compiler_params=_cparams(("parallel", "arbitrary")),
        name="hgrn_mixer",
    )(proj, proj, proj, proj, lb.reshape(1, MIX_HEADS_W), nw.reshape(1, HEAD_DIM))


def _lane_pick(x, lane_ids, idx):
    return jnp.sum(jnp.where(lane_ids == idx, x, 0.0), axis=1, keepdims=True)


def _gdn_kernel(q_ref, k_ref, v_ref, z_ref, ab_ref, cq_ref, ck_ref, cv_ref, alog_ref, dtb_ref, nw_ref,
                o_ref, st_ref, xq_ref, xk_ref, xv_ref, *, nchunk, tb, hpb):
    head0 = pl.program_id(0) * hpb
    tail = 8
    width = hpb * HEAD_DIM

    @pl.when(pl.program_id(1) == 0)
    def _():
        st_ref[...] = jnp.zeros_like(st_ref)
        for xr in (xq_ref, xk_ref, xv_ref):
            xr[0:tail, :] = jnp.zeros((tail, width), F32)

    for src, xr, cw in ((q_ref, xq_ref, cq_ref), (k_ref, xk_ref, ck_ref), (v_ref, xv_ref, cv_ref)):
        xr[tail:tail + tb, :] = src[...]
        acc = jnp.zeros((tb, width), F32)
        for j in range(CONV_K):
            acc = acc + cw[j:j + 1, :] * xr[pl.ds(tail - (CONV_K - 1) + j, tb), :]
        new_tail = xr[tb:tb + tail, :]
        xr[tail:tail + tb, :] = _silu(acc)
        xr[0:tail, :] = new_tail

    nw = nw_ref[...]
    r, c = _chunk_masks()
    incl = c <= r
    strict = c < r
    eye = c == r
    tri = incl.astype(BF16)
    ones = jnp.ones((CHUNK, CHUNK), BF16)
    lane = lax.broadcasted_iota(jnp.int32, (CHUNK, LANES), 1)
    neg_a = -jnp.exp(alog_ref[...])
    dtb = dtb_ref[...]

    def head_chunk(sl, slx, hd, cum_all, beta_all):
        hs = slice(hd * HEAD_DIM, (hd + 1) * HEAD_DIM)
        qc = xq_ref[slx, hs]
        kc = xk_ref[slx, hs]
        v = xv_ref[slx, hs]
        q = qc * lax.rsqrt(jnp.sum(qc * qc, axis=-1, keepdims=True) + 1e-6) * (HEAD_DIM ** -0.5)
        k = kc * lax.rsqrt(jnp.sum(kc * kc, axis=-1, keepdims=True) + 1e-6)
        cum = _lane_pick(cum_all, lane, head0 + hd)
        beta = _lane_pick(beta_all, lane, head0 + hd + N_HEADS)
        cum_b = jnp.broadcast_to(cum, (CHUNK, CHUNK))
        cum_row = _dot_exact_lhs(ones, jnp.where(eye, cum_b, 0.0))
        kk = _dot_nt(k, k)
        qk = _dot_nt(q, k)
        yield
        gam = jnp.where(incl, jnp.exp(jnp.where(incl, cum_b - cum_row, 0.0)), 0.0)
        last = cum[CHUNK - 1:CHUNK, :]
        ecum = jnp.exp(cum)
        a = jnp.where(strict, kk * gam, 0.0) * beta
        eyef = eye.astype(F32)
        tinv = eyef - a
        pw = _dot(a, a)
        yield
        for _ in range(int(math.log2(CHUNK)) - 2):
            tinv, pw = tinv + _dot(tinv, pw), _dot(pw, pw)
            yield
        tinv = tinv + _dot(tinv, pw)
        yield
        rhs = jnp.concatenate([v * beta, k * (beta * ecum)], axis=1)
        sol = _dot(tinv, rhs)
        qe = q * ecum
        qkg = qk * gam
        kdec = k * jnp.exp(last - cum)
        elast = jnp.exp(last)
        yield _STATE
        u = sol[:, :HEAD_DIM]
        w = sol[:, HEAD_DIM:]
        st = st_ref[hd]
        o = _dot(qe, st)
        v_new = u - _dot(w, st)
        yield
        o = o + _dot(qkg, v_new)
        st_ref[hd] = st * elast + _dot_tn(kdec, v_new)
        yield
        o_ref[sl, hs] = (_rms(o, nw) * _silu(z_ref[sl, hs])).astype(o_ref.dtype)

    def chunk_heads(off):
        sl = pl.ds(off, CHUNK)
        slx = pl.ds(off + tail, CHUNK)
        gates = ab_ref[sl, :]
        cum_all = _dot_exact_lhs(tri, neg_a * jax.nn.softplus(gates + dtb))
        beta_all = jax.nn.sigmoid(gates)
        return [head_chunk(sl, slx, hd, cum_all, beta_all) for hd in range(hpb)]

    def body(ci, carry):
        off = pl.multiple_of(ci * (2 * CHUNK), 2 * CHUNK)
        _run_chunk_pair(chunk_heads(off), chunk_heads(pl.multiple_of(off + CHUNK, CHUNK)))
        return carry

    lax.fori_loop(0, nchunk // 2, body, 0)


def gdn_mixer(proj, gates, gate_blk, conv_w, a_log, dt_bias, nw, *, tb, hpb):
    t = proj.shape[0]
    nchunk = tb // CHUNK
    ngrp = N_HEADS // hpb
    w = hpb * HEAD_DIM

    def col(part):
        return pl.BlockSpec((tb, w), lambda g, i, part=part: (i, part * ngrp + g))

    def cw(part):
        return pl.BlockSpec((CONV_K, w), lambda g, i, part=part: (0, part * ngrp + g))

    def row_pad(vec):
        return jnp.zeros((1, LANES), F32).at[0, :N_HEADS].set(vec.astype(F32))

    return pl.pallas_call(
        functools.partial(_gdn_kernel, nchunk=nchunk, tb=tb, hpb=hpb),
        grid=(ngrp, t // tb),
        in_specs=[col(4), col(5), col(6), col(7),
                  pl.BlockSpec((tb, LANES), lambda g, i: (i, gate_blk)),
                  cw(0), cw(1), cw(2),
                  pl.BlockSpec((1, LANES), lambda g, i: (0, 0)),
                  pl.BlockSpec((1, LANES), lambda g, i: (0, 0)),
                  pl.BlockSpec((1, HEAD_DIM), lambda g, i: (0, 0))],
        out_specs=pl.BlockSpec((tb, w), lambda g, i: (i, g)),
        out_shape=jax.ShapeDtypeStruct((t, MIX_HEADS_W), BF16),
        scratch_shapes=[pltpu.VMEM((hpb, HEAD_DIM, HEAD_DIM), F32)]
        + [pltpu.VMEM((tb + 8, w), F32) for _ in range(3)],
        compiler_params=_cparams(("parallel", "arbitrary")),
        name="gdn_mixer",
    )(proj, proj, proj, proj, gates, conv_w, conv_w, conv_w, row_pad(a_log), row_pad(dt_bias),
      nw.reshape(1, HEAD_DIM))


_S5_PW = S5_L * 2 * S5_CH
_S5_SW = 2 * S5_STATE


def _dot3(a, b):
    ah = a.astype(BF16)
    al = (a - ah.astype(F32)).astype(BF16)
    bh = b.astype(BF16)
    bl = (b - bh.astype(F32)).astype(BF16)
    return (jnp.dot(ah, bh, preferred_element_type=F32) + jnp.dot(ah, bl, preferred_element_type=F32)
            + jnp.dot(al, bh, preferred_element_type=F32))


def _s5_tables_kernel(arc_ref, aic_ref, lsc_ref, arr_ref, air_ref, lsr_ref, cre_ref, cim_ref, bre_ref, bim_ref,
                      wcat_ref, ws2o_ref, alr_ref, ali_ref):
    are = jnp.minimum(arc_ref[0], -1e-4)
    aim = aic_ref[0]
    delta = jnp.exp(lsc_ref[0])
    lane = lax.broadcasted_iota(jnp.int32, (1, _S5_PW), 1)
    tau = lax.shift_right_logical(lane, 5).astype(F32)
    mag = jnp.exp(are * delta * tau)
    ang = aim * delta * tau
    lr = mag * jnp.cos(ang)
    li = mag * jnp.sin(ang)
    cre = cre_ref[0]
    cim = cim_ref[0]
    rr0 = jnp.concatenate([lr * cre - li * cim, -(lr * cim + li * cre)], axis=0)
    m1 = jnp.exp(are * delta)
    l1r = m1 * jnp.cos(aim * delta)
    l1i = m1 * jnp.sin(aim * delta)
    lr1 = lr * l1r - li * l1i
    li1 = lr * l1i + li * l1r
    ws2o_ref[0] = jnp.concatenate([lr1 * cre - li1 * cim, -(lr1 * cim + li1 * cre)], axis=0).astype(BF16)

    ar = jnp.minimum(arr_ref[0], -1e-4)
    ai = air_ref[0]
    dl = jnp.exp(lsr_ref[0])
    mb = jnp.exp(ar * dl)
    lbr = mb * jnp.cos(ai * dl)
    lbi = mb * jnp.sin(ai * dl)
    den = ar * ar + ai * ai
    xr = lbr - 1.0
    cr = (xr * ar + lbi * ai) / den
    ci = (lbi * ar - xr * ai) / den
    bre = bre_ref[0]
    bim = bim_ref[0]
    bbr = cr * bre - ci * bim
    bbi = cr * bim + ci * bre
    g = _dot3(jnp.concatenate([bbr, bbi], axis=1), rr0)
    rows_per = 2 * S5_CH
    for s in range(S5_L):
        if s == 0:
            blk = g
        else:
            blk = jnp.where(lane >= rows_per * s, pltpu.roll(g, rows_per * s, axis=1), 0.0)
        wcat_ref[0, rows_per * s:rows_per * (s + 1), 0:_S5_PW] = blk.astype(BF16)
    row = lax.broadcasted_iota(jnp.int32, (_S5_PW, 1), 0)
    mpow = (S5_L - 1 - lax.shift_right_logical(row, 5)).astype(F32)
    pm = jnp.exp(ar * dl * mpow)
    pr = pm * jnp.cos(ai * dl * mpow)
    pi = pm * jnp.sin(ai * dl * mpow)
    bbr_t = jnp.concatenate([bbr] * S5_L, axis=0)
    bbi_t = jnp.concatenate([bbi] * S5_L, axis=0)
    wcat_ref[0, :, _S5_PW:_S5_PW + _S5_SW] = (pr * bbr_t - pi * bbi_t).astype(BF16)
    wcat_ref[0, :, _S5_PW + _S5_SW:] = (pr * bbi_t + pi * bbr_t).astype(BF16)
    ml = jnp.exp(ar * dl * S5_L)
    alr_ref[0] = ml * jnp.cos(ai * dl * S5_L)
    ali_ref[0] = ml * jnp.sin(ai * dl * S5_L)


def s5_tables(a_re, a_im, b_re, b_im, c_re, c_im, log_step):
    p = S5_PAIRS
    ls = jnp.broadcast_to(log_step.astype(F32)[:, None], (S5_GROUPS, S5_STATE))
    same = (jnp.eye(2, dtype=F32) > 0).reshape(1, 2, 2, 1, 1)

    def ctile(cm):
        x = jnp.where(same, cm.astype(F32).reshape(p, 1, 2, S5_CH, S5_STATE), 0.0)
        x = x.transpose(0, 1, 4, 2, 3)[:, :, :, None]
        return jnp.broadcast_to(x, (p, 2, S5_STATE, S5_L, 2, S5_CH)).reshape(p, _S5_SW, _S5_PW)

    def btile(bm):
        x = jnp.where(same, bm.astype(F32).reshape(p, 2, 1, S5_STATE, S5_CH), 0.0)
        return x.transpose(0, 1, 4, 2, 3).reshape(p, 2 * S5_CH, _S5_SW)

    col = lambda v: v.astype(F32).reshape(p, _S5_SW, 1)
    rowv = lambda v: v.astype(F32).reshape(p, 1, _S5_SW)
    spec = lambda shape: pl.BlockSpec((1,) + shape, lambda i: (i, 0, 0))
    return pl.pallas_call(
        _s5_tables_kernel,
        grid=(p,),
        in_specs=[spec((_S5_SW, 1))] * 3 + [spec((1, _S5_SW))] * 3
        + [spec((_S5_SW, _S5_PW))] * 2 + [spec((2 * S5_CH, _S5_SW))] * 2,
        out_specs=[spec((_S5_PW, _S5_PW + 2 * _S5_SW)), spec((2 * _S5_SW, _S5_PW)),
                   spec((1, _S5_SW)), spec((1, _S5_SW))],
        out_shape=[jax.ShapeDtypeStruct((p, _S5_PW, _S5_PW + 2 * _S5_SW), BF16),
                   jax.ShapeDtypeStruct((p, 2 * _S5_SW, _S5_PW), BF16),
                   jax.ShapeDtypeStruct((p, 1, _S5_SW), F32),
                   jax.ShapeDtypeStruct((p, 1, _S5_SW), F32)],
        compiler_params=_cparams(("parallel",)),
        name="s5_tables",
    )(col(a_re), col(a_im), col(ls), rowv(a_re), rowv(a_im), rowv(ls),
      ctile(c_re), ctile(c_im), btile(b_re), btile(b_im))


def _s5_in_kernel(u_ref, wcat_ref, d_ref, yi_ref, vre_ref, vim_ref):
    u = u_ref[0]
    ycat = jnp.dot(u.astype(BF16), wcat_ref[0], preferred_element_type=F32)
    yi_ref[0] = ycat[:, :_S5_PW] + d_ref[0] * u
    vre_ref[...] = ycat[:, _S5_PW:_S5_PW + _S5_SW]
    vim_ref[...] = ycat[:, _S5_PW + _S5_SW:]


def _s5_scan_kernel(vre_ref, vim_ref, alr_ref, ali_ref, xre_ref, xim_ref, *, nsteps):
    ar = alr_ref[...]
    ai = ali_ref[...]

    def body(j, carry):
        xr, xi = carry
        xre_ref[j] = xr
        xim_ref[j] = xi
        return (ar * xr - ai * xi + vre_ref[j], ar * xi + ai * xr + vim_ref[j])

    zero = jnp.zeros(ar.shape, F32)
    lax.fori_loop(0, nsteps, body, (zero, zero))


def _gelu_tanh(x):
    return 0.5 * x * (1.0 + jnp.tanh(math.sqrt(2.0 / math.pi) * (x + 0.044715 * (x * x * x))))


def _s5_out_kernel(yi_ref, xre_ref, xim_ref, ws2o_ref, y_ref):
    w = ws2o_ref[0]
    y = (yi_ref[0] + jnp.dot(xre_ref[...].astype(BF16), w[:_S5_SW], preferred_element_type=F32)
         + jnp.dot(xim_ref[...].astype(BF16), w[_S5_SW:], preferred_element_type=F32))
    y_ref[0] = _gelu_tanh(y)


def _glu_kernel(y_ref, w_ref, b_ref, o_ref):
    y = y_ref[...]
    z = jnp.dot(y.astype(BF16), w_ref[...], preferred_element_type=F32) + b_ref[...]
    o_ref[...] = (y * jax.nn.sigmoid(z)).astype(o_ref.dtype)


def s5_mixer(u, a_re, a_im, b_re, b_im, c_re, c_im, d_skip, log_step, w_glu, b_glu):
    t = u.shape[0]
    nj = t // S5_L
    p = S5_PAIRS
    wcat, ws2o, alr, ali = s5_tables(a_re, a_im, b_re, b_im, c_re, c_im, log_step)
    up = u.reshape(nj, S5_L, p, 2 * S5_CH).transpose(2, 0, 1, 3).reshape(p, nj, _S5_PW)
    dt = jnp.tile(d_skip.astype(F32).reshape(p, 1, 2 * S5_CH), (1, 1, S5_L))
    pair3 = lambda shape: pl.BlockSpec((1,) + shape, lambda i: (i, 0, 0))
    lanes = lambda: pl.BlockSpec((nj, _S5_SW), lambda i: (0, i))
    yi, vre, vim = pl.pallas_call(
        _s5_in_kernel,
        grid=(p,),
        in_specs=[pair3((nj, _S5_PW)), pair3((_S5_PW, _S5_PW + 2 * _S5_SW)), pair3((1, _S5_PW))],
        out_specs=[pair3((nj, _S5_PW)), lanes(), lanes()],
        out_shape=[jax.ShapeDtypeStruct((p, nj, _S5_PW), F32),
                   jax.ShapeDtypeStruct((nj, p * _S5_SW), F32),
                   jax.ShapeDtypeStruct((nj, p * _S5_SW), F32)],
        compiler_params=_cparams(("parallel",)),
        name="s5_in",
    )(up, wcat, dt)
    sw = p * _S5_SW
    tile = (8, sw // 8)
    lb = tile[1] // 2
    sblk = lambda: pl.BlockSpec((nj, 8, lb), lambda i: (0, 0, i))
    ablk = lambda: pl.BlockSpec((8, lb), lambda i: (0, i))
    xre, xim = pl.pallas_call(
        functools.partial(_s5_scan_kernel, nsteps=nj),
        grid=(2,),
        in_specs=[sblk(), sblk(), ablk(), ablk()],
        out_specs=[sblk(), sblk()],
        out_shape=[jax.ShapeDtypeStruct((nj,) + tile, F32)] * 2,
        compiler_params=_cparams(("parallel",)),
        name="s5_scan",
    )(vre.reshape((nj,) + tile), vim.reshape((nj,) + tile), alr.reshape(tile), ali.reshape(tile))
    yp = pl.pallas_call(
        _s5_out_kernel,
        grid=(p,),
        in_specs=[pair3((nj, _S5_PW)), lanes(), lanes(), pair3((2 * _S5_SW, _S5_PW))],
        out_specs=pair3((nj, _S5_PW)),
        out_shape=jax.ShapeDtypeStruct((p, nj, _S5_PW), F32),
        compiler_params=_cparams(("parallel",)),
        name="s5_out",
    )(yi, xre.reshape(nj, sw), xim.reshape(nj, sw), ws2o)
    y = yp.reshape(p, nj, S5_L, 2 * S5_CH).transpose(1, 2, 0, 3).reshape(t, S5_W)
    tm = min(t, 1024)
    return pl.pallas_call(
        _glu_kernel,
        grid=(t // tm,),
        in_specs=[pl.BlockSpec((tm, S5_W), lambda i: (i, 0)),
                  pl.BlockSpec((S5_W, S5_W), lambda i: (0, 0)),
                  pl.BlockSpec((1, S5_W), lambda i: (0, 0))],
        out_specs=pl.BlockSpec((tm, S5_W), lambda i: (i, 0)),
        out_shape=jax.ShapeDtypeStruct((t, S5_W), BF16),
        compiler_params=_cparams(("parallel",)),
        name="s5_glu",
    )(y, w_glu.astype(BF16), b_glu.astype(F32).reshape(1, S5_W))


def _lower_bounds_kernel(x_ref, o_ref):
    x = x_ref[...]
    e = jnp.exp(x - jnp.max(x, axis=0, keepdims=True))
    p = e / jnp.sum(e, axis=0, keepdims=True)
    run = p[0:1]
    rows = [run - p[0:1]]
    for i in range(1, x.shape[0]):
        run = run + p[i:i + 1]
        rows.append(run - p[0:1])
    o_ref[...] = jnp.concatenate(rows, axis=0)


def lower_bounds(logits):
    return pl.pallas_call(
        _lower_bounds_kernel,
        out_shape=jax.ShapeDtypeStruct(logits.shape, F32),
        name="hgrn_lower_bounds",
    )(logits.astype(F32))


def _out_proj_kernel(h_ref, oa_ref, ob_ref, yc_ref, w1_ref, w2_ref, w3_ref, o_ref):
    o_ref[...] = (h_ref[...]
                  + jnp.dot(oa_ref[...], w1_ref[0].astype(BF16), preferred_element_type=F32)
                  + jnp.dot(ob_ref[...], w2_ref[0].astype(BF16), preferred_element_type=F32)
                  + jnp.dot(yc_ref[...], w3_ref[0].astype(BF16), preferred_element_type=F32))


def out_proj(h, oa, ob, yc, w, layer, *, tm, tn):
    m, n = h.shape
    hw = MIX_HEADS_W
    return pl.pallas_call(
        _out_proj_kernel,
        grid=(m // tm, n // tn),
        in_specs=[pl.BlockSpec((tm, tn), lambda i, j: (i, j)),
                  pl.BlockSpec((tm, hw), lambda i, j: (i, 0)),
                  pl.BlockSpec((tm, hw), lambda i, j: (i, 0)),
                  pl.BlockSpec((tm, S5_W), lambda i, j: (i, 0)),
                  pl.BlockSpec((1, hw, tn), lambda i, j: (layer, 0, j)),
                  pl.BlockSpec((1, hw, tn), lambda i, j: (layer, 1, j)),
                  pl.BlockSpec((1, S5_W, tn), lambda i, j: (layer, 2 * hw // S5_W, j))],
        out_specs=pl.BlockSpec((tm, tn), lambda i, j: (i, j)),
        out_shape=jax.ShapeDtypeStruct((m, n), F32),
        compiler_params=_cparams(("parallel", "arbitrary")),
        name="out_proj",
    )(h, oa, ob, yc, w, w, w)


def _xattn_kernel(h_ref, nw_ref, wq_ref, kv_ref, wo_ref, o_ref, wqb_ref, wob_ref):
    @pl.when(pl.program_id(0) == 0)
    def _():
        wqb_ref[...] = wq_ref[0].astype(BF16)
        wob_ref[...] = wo_ref[0].astype(BF16)

    h = h_ref[...]
    q = jnp.dot(_rms(h, nw_ref[...]).astype(BF16), wqb_ref[...], preferred_element_type=F32)
    kv = kv_ref[...]
    outs = []
    for hd in range(XA_HEADS):
        lo, hi = hd * XA_DH, (hd + 1) * XA_DH
        s = _dot_nt(q[:, lo:hi], kv[:, lo:hi]) * (XA_DH ** -0.5)
        e = jnp.exp(s - jnp.max(s, axis=-1, keepdims=True))
        p = e / jnp.sum(e, axis=-1, keepdims=True)
        outs.append(_dot(p, kv[:, XA_W + lo:XA_W + hi]))
    o = jnp.concatenate(outs, axis=1)
    o_ref[...] = h + _dot(o, wob_ref[...])


def xattn(h, nw, wq, kv, wo, layer, *, tm):
    m, d = h.shape
    nw = nw.reshape(1, d)
    return pl.pallas_call(
        _xattn_kernel,
        grid=(m // tm,),
        in_specs=[pl.BlockSpec((tm, d), lambda i: (i, 0)),
                  pl.BlockSpec((1, d), lambda i: (0, 0)),
                  pl.BlockSpec((1, d, XA_W), lambda i: (layer, 0, 0)),
                  pl.BlockSpec(kv.shape, lambda i: (0, 0)),
                  pl.BlockSpec((1, XA_W, d), lambda i: (layer, 0, 0))],
        out_specs=pl.BlockSpec((tm, d), lambda i: (i, 0)),
        out_shape=jax.ShapeDtypeStruct((m, d), F32),
        scratch_shapes=[pltpu.VMEM((d, XA_W), BF16), pltpu.VMEM((XA_W, d), BF16)],
        compiler_params=_cparams(("arbitrary",)),
        name="xattn",
    )(h, nw, wq, kv, wo)


def _ffn_kernel(te_ref, nv_ref, x_ref, nw_ref, wg_ref, wu_ref, wd_ref, o_ref, xn_ref, *, norm, residual, nf, lead):
    i = pl.program_id(0)
    f = pl.program_id(1)
    widx = (0,) * (lead + 1)

    @pl.when(i < nv_ref[0])
    def _():
        @pl.when(f == 0)
        def _():
            x = x_ref[...]
            xn_ref[...] = (_rms(x, nw_ref[...]) if norm else x).astype(BF16)
            o_ref[...] = x if residual else jnp.zeros_like(o_ref)

        xn = xn_ref[...]
        g = jnp.dot(xn, wg_ref[widx].astype(BF16), preferred_element_type=F32)
        u = jnp.dot(xn, wu_ref[widx].astype(BF16), preferred_element_type=F32)
        o_ref[...] += jnp.dot((_silu(g) * u).astype(BF16), wd_ref[widx].astype(BF16), preferred_element_type=F32)

    @pl.when((i >= nv_ref[0]) & (f == nf - 1))
    def _():
        o_ref[...] = jnp.zeros_like(o_ref)


def ffn(x, nw, wg, wu, wd, lead, tile_expert, nvalid, *, tm, tf, norm, residual):
    m, d = x.shape
    dff = wg.shape[-1]
    nf = dff // tf
    lead = tuple(lead)
    ones = (1,) * (len(lead) + 1)

    def row(i, f, te, nv):
        return (jnp.minimum(i, nv[0] - 1), 0)

    def wcol(i, f, te, nv):
        return lead + (te[jnp.minimum(i, nv[0] - 1)], 0, jnp.where(i < nv[0], f, nf - 1))

    def wrow(i, f, te, nv):
        return lead + (te[jnp.minimum(i, nv[0] - 1)], jnp.where(i < nv[0], f, nf - 1), 0)

    return pl.pallas_call(
        functools.partial(_ffn_kernel, norm=norm, residual=residual, nf=nf, lead=len(lead)),
        grid_spec=pltpu.PrefetchScalarGridSpec(
            num_scalar_prefetch=2,
            grid=(m // tm, nf),
            in_specs=[pl.BlockSpec((tm, d), row),
                      pl.BlockSpec((1, d), lambda i, f, te, nv: (0, 0)),
                      pl.BlockSpec(ones + (d, tf), wcol),
                      pl.BlockSpec(ones + (d, tf), wcol),
                      pl.BlockSpec(ones + (tf, d), wrow)],
            out_specs=pl.BlockSpec((tm, d), lambda i, f, te, nv: (i, 0)),
            scratch_shapes=[pltpu.VMEM((tm, d), BF16)]),
        out_shape=jax.ShapeDtypeStruct((m, d), F32),
        compiler_params=_cparams(("arbitrary", "arbitrary")),
        name="ffn",
    )(tile_expert, nvalid, x, nw.reshape(1, d), wg, wu, wd)


def _router_kernel(h_ref, nw_ref, rw_ref, hn_ref, info_ref, cnt_ref, carry_ref, tri_ref, *, tm):
    @pl.when(pl.program_id(0) == 0)
    def _():
        carry_ref[...] = jnp.zeros_like(carry_ref)
        r = lax.broadcasted_iota(jnp.int32, (tm, tm), 0)
        c = lax.broadcasted_iota(jnp.int32, (tm, tm), 1)
        tri_ref[...] = (c < r).astype(BF16)

    hn = _rms(h_ref[...], nw_ref[...])
    hn_ref[...] = hn
    logits = _dot3(hn, rw_ref[...])
    lane = lax.broadcasted_iota(jnp.int32, (tm, LANES), 1).astype(F32)
    neg = -jnp.inf
    lg = jnp.where(lane < N_EXPERTS, logits, neg)
    m1 = jnp.max(lg, axis=1, keepdims=True)
    i1 = jnp.min(jnp.where(lg == m1, lane, float(LANES)), axis=1, keepdims=True)
    lg2 = jnp.where(lane == i1, neg, lg)
    m2 = jnp.max(lg2, axis=1, keepdims=True)
    i2 = jnp.min(jnp.where(lg2 == m2, lane, float(LANES)), axis=1, keepdims=True)
    e2 = jnp.exp(m2 - m1)
    g1 = 1.0 / (1.0 + e2)
    g2 = e2 / (1.0 + e2)
    cnt = jnp.where((lane == i1) | (lane == i2), 1.0, 0.0)
    carry = carry_ref[...]
    before = jnp.dot(tri_ref[...], cnt.astype(BF16), preferred_element_type=F32) + carry
    r1 = jnp.sum(jnp.where(lane == i1, before, 0.0), axis=1, keepdims=True)
    r2 = jnp.sum(jnp.where(lane == i2, before, 0.0), axis=1, keepdims=True)
    carry = carry + jnp.sum(cnt, axis=0, keepdims=True)
    carry_ref[...] = carry
    cnt_ref[...] = jnp.broadcast_to(carry, cnt_ref.shape)
    info = jnp.zeros((tm, LANES), F32)
    for ln, val in enumerate((i1, i2, r1, r2, g1, g2)):
        info = jnp.where(lane == ln, val, info)
    info_ref[...] = info


def router(h, nw, rw, *, tm):
    m, d = h.shape
    rwp = jnp.zeros((d, LANES), F32).at[:, :N_EXPERTS].set(rw.astype(F32))
    return pl.pallas_call(
        functools.partial(_router_kernel, tm=tm),
        grid=(m // tm,),
        in_specs=[pl.BlockSpec((tm, d), lambda i: (i, 0)),
                  pl.BlockSpec((1, d), lambda i: (0, 0)),
                  pl.BlockSpec((d, LANES), lambda i: (0, 0))],
        out_specs=[pl.BlockSpec((tm, d), lambda i: (i, 0)),
                   pl.BlockSpec((tm, LANES), lambda i: (i, 0)),
                   pl.BlockSpec((8, LANES), lambda i: (0, 0))],
        out_shape=[jax.ShapeDtypeStruct((m, d), F32),
                   jax.ShapeDtypeStruct((m, LANES), F32),
                   jax.ShapeDtypeStruct((8, LANES), F32)],
        scratch_shapes=[pltpu.VMEM((1, LANES), F32), pltpu.VMEM((tm, tm), BF16)],
        compiler_params=_cparams(("arbitrary",)),
        name="moe_router",
    )(h, nw.reshape(1, d), rwp)


def _dispatch_kernel(dest_ref, hn_ref, init_ref, xb_ref, sem, *, tm):
    del init_ref
    base = pl.program_id(0) * tm

    def row_copy(t, d):
        return pltpu.make_async_copy(hn_ref.at[pl.ds(t, 1)], xb_ref.at[pl.ds(d, 1)], sem)

    def issue(t, c):
        row_copy(t, dest_ref[2 * (base + t)]).start()
        row_copy(t, dest_ref[2 * (base + t) + 1]).start()
        return c

    def drain(t, c):
        row_copy(0, 0).wait()
        row_copy(0, 0).wait()
        return c

    lax.fori_loop(0, tm, issue, 0)
    lax.fori_loop(0, tm, drain, 0)


def dispatch(hn, dest, rows, *, tm):
    m, d = hn.shape
    return pl.pallas_call(
        functools.partial(_dispatch_kernel, tm=tm),
        grid_spec=pltpu.PrefetchScalarGridSpec(
            num_scalar_prefetch=1,
            grid=(m // tm,),
            in_specs=[pl.BlockSpec((tm, d), lambda i, dst: (i, 0)),
                      pl.BlockSpec(memory_space=pl.ANY)],
            out_specs=pl.BlockSpec(memory_space=pl.ANY),
            scratch_shapes=[pltpu.SemaphoreType.DMA(())]),
        out_shape=jax.ShapeDtypeStruct((rows, d), F32),
        input_output_aliases={2: 0},
        compiler_params=_cparams(("arbitrary",)),
        name="moe_dispatch",
    )(dest, hn, jnp.zeros((rows, d), F32))


def _combine_kernel(dest_ref, h_ref, info_ref, yb_ref, o_ref, buf_ref, sem, *, tm):
    i = pl.program_id(0)
    n = pl.num_programs(0)

    def row_copy(t, d, slot, k):
        return pltpu.make_async_copy(yb_ref.at[pl.ds(d, 1)], buf_ref.at[slot, k, pl.ds(t, 1)], sem.at[slot])

    def gather(tile, slot):
        base = tile * tm

        def issue(t, c):
            row_copy(t, dest_ref[2 * (base + t)], slot, 0).start()
            row_copy(t, dest_ref[2 * (base + t) + 1], slot, 1).start()
            return c

        lax.fori_loop(0, tm, issue, 0)

    @pl.when(i == 0)
    def _():
        gather(0, 0)

    @pl.when(i + 1 < n)
    def _():
        gather(i + 1, (i + 1) % 2)

    slot = i % 2

    def drain(t, c):
        row_copy(0, 0, slot, 0).wait()
        row_copy(0, 0, slot, 1).wait()
        return c

    lax.fori_loop(0, tm, drain, 0)
    info = info_ref[...]
    o_ref[...] = h_ref[...] + (info[:, 4:5] * buf_ref[slot, 0] + info[:, 5:6] * buf_ref[slot, 1])


def combine(h, info, yb, dest, *, tm):
    m, d = h.shape
    return pl.pallas_call(
        functools.partial(_combine_kernel, tm=tm),
        grid_spec=pltpu.PrefetchScalarGridSpec(
            num_scalar_prefetch=1,
            grid=(m // tm,),
            in_specs=[pl.BlockSpec((tm, d), lambda i, dst: (i, 0)),
                      pl.BlockSpec((tm, LANES), lambda i, dst: (i, 0)),
                      pl.BlockSpec(memory_space=pl.ANY)],
            out_specs=pl.BlockSpec((tm, d), lambda i, dst: (i, 0)),
            scratch_shapes=[pltpu.VMEM((2, 2, tm, d), F32), pltpu.SemaphoreType.DMA((2,))]),
        out_shape=jax.ShapeDtypeStruct((m, d), F32),
        compiler_params=_cparams(("arbitrary",)),
        name="moe_combine",
    )(dest, h, info, yb)


def moe_layer(h, nw, rw, wg, wu, wd, lead):
    m, d = h.shape
    tm = MOE_TM
    ntiles = m * 2 // tm + N_EXPERTS
    hn, info, cnt = router(h, nw, rw, tm=min(m, ROW_TM))
    counts = cnt[0, :N_EXPERTS].astype(jnp.int32)
    seg = (counts + tm - 1) // tm * tm
    seg_end = jnp.cumsum(seg)
    seg_start = seg_end - seg
    ids = info[:, 0:2].astype(jnp.int32)
    dest = (seg_start[ids] + info[:, 2:4].astype(jnp.int32)).reshape(2 * m)
    tile_start = jnp.arange(ntiles, dtype=jnp.int32) * tm
    tile_expert = jnp.minimum(
        jnp.sum((seg_end[None, :] <= tile_start[:, None]).astype(jnp.int32), axis=1), N_EXPERTS - 1)
    nvalid = (seg_end[-1:] // tm).astype(jnp.int32)
    xb = dispatch(hn, dest, ntiles * tm, tm=256)
    yb = ffn(xb, nw, wg, wu, wd, lead, tile_expert, nvalid, tm=tm, tf=MOE_TF, norm=False, residual=False)
    return combine(h, info, yb, dest, tm=256)


def _final_norm_kernel(x_ref, w_ref, o_ref):
    o_ref[...] = _rms(x_ref[...], w_ref[...])


def final_norm(h, w, *, tm):
    m, d = h.shape
    return pl.pallas_call(
        _final_norm_kernel,
        grid=(m // tm,),
        in_specs=[pl.BlockSpec((tm, d), lambda i: (i, 0)), pl.BlockSpec((1, d), lambda i: (0, 0))],
        out_specs=pl.BlockSpec((tm, d), lambda i: (i, 0)),
        out_shape=jax.ShapeDtypeStruct((m, d), F32),
        compiler_params=_cparams(("parallel",)),
        name="final_norm",
    )(h, w.reshape(1, d))


def kernel(x, mem, norm_mix, w_in, w_out, hgrn_lb_logits, hgrn_norm, gdn_conv, gdn_a_log, gdn_dt_bias, gdn_norm, s5_a_re, s5_a_im, s5_b_re, s5_b_im, s5_c_re, s5_c_im, s5_d, s5_log_step, s5_w_glu, s5_b_glu, norm_cross, norm_mem, xa_wq, xa_wk, xa_wv, xa_wo, norm_ffn, ffn_w_gate, ffn_w_up, ffn_w_down, moe_router, moe_w_gate, moe_w_up, moe_w_down, norm_final):
    bsz, t, d = x.shape
    assert bsz == 1
    depth = w_in.shape[0]
    h = x[0].astype(F32)
    memx = mem[0].astype(F32)
    lbs = lower_bounds(hgrn_lb_logits)
    tb = min(t, 512)
    tmr = min(t, 1024)
    dense_tm = min(t, DENSE_TM)
    dense_tiles = t // dense_tm
    w_kv = jnp.concatenate([xa_wk, xa_wv], axis=2)
    ngate = 2 * N_HEADS
    for layer in range(depth):
        proj = norm_matmul(h, norm_mix[layer], w_in, layer, 0, IN_WIDE, tm=tmr, tn=512)
        rest = norm_matmul(h, norm_mix[layer], w_in, layer, IN_WIDE, LANES, tm=tmr, tn=LANES)
        oa = hgrn_mixer(proj, lbs[layer], hgrn_norm[layer], tb=tb, hpb=HEADS_PER_BODY)
        ob = gdn_mixer(proj, proj, IN_MAIN // LANES, gdn_conv[layer].astype(F32), gdn_a_log[layer],
                       gdn_dt_bias[layer], gdn_norm[layer], tb=tb, hpb=HEADS_PER_BODY)
        u_s5 = jnp.concatenate([proj[:, IN_MAIN + ngate:], rest[:, :IN_MAIN + ngate + S5_W - IN_WIDE]], axis=1)
        yc = s5_mixer(u_s5, s5_a_re[layer], s5_a_im[layer], s5_b_re[layer],
                      s5_b_im[layer], s5_c_re[layer], s5_c_im[layer], s5_d[layer], s5_log_step[layer],
                      s5_w_glu[layer], s5_b_glu[layer])
        h = out_proj(h, oa, ob, yc, w_out, layer, tm=tmr, tn=512)
        kv = norm_matmul(memx, norm_mem[layer], w_kv, layer, 0, 2 * XA_W, tm=memx.shape[0], tn=512,
                         out_dtype=BF16)
        h = xattn(h, norm_cross[layer], xa_wq, kv, xa_wo, layer, tm=min(t, ROW_TM))
        i = layer // 2
        if layer % 2 == 0:
            h = ffn(h, norm_ffn[layer], ffn_w_gate, ffn_w_up, ffn_w_down, (), jnp.full((dense_tiles,), i, jnp.int32),
                    jnp.full((1,), dense_tiles, jnp.int32), tm=dense_tm, tf=DENSE_TF, norm=True, residual=True)
        else:
            h = moe_layer(h, norm_ffn[layer], moe_router[i], moe_w_gate, moe_w_up, moe_w_down, (i,))
    return final_norm(h, norm_final, tm=tmr)[None].astype(x.dtype)
```

```python
import functools
import math

import jax
import jax.numpy as jnp
from jax import lax
from jax.experimental import pallas as pl
from jax.experimental.pallas import tpu as pltpu

F32 = jnp.float32
BF16 = jnp.bfloat16

D_MODEL = 2048
N_HEADS = 6
HEAD_DIM = 128
CHUNK = 64
SUB = 16
CONV_K = 4
S5_GROUPS = 32
S5_CH = 16
S5_STATE = 64
S5_W = S5_GROUPS * S5_CH
S5_L = 16
S5_PAIRS = S5_GROUPS // 2
XA_HEADS = 4
XA_DH = 128
XA_W = XA_HEADS * XA_DH
D_FF = 5632
N_EXPERTS = 8
NORM_EPS = 1e-6
LB_FLOOR = 1e-30
LANES = 128
MIX_HEADS_W = N_HEADS * HEAD_DIM
IN_MAIN = 8 * MIX_HEADS_W
IN_WIDE = IN_MAIN + S5_W
HEADS_PER_BODY = 6
ROW_TM = 512
MOE_TM = 768
MOE_TF = 512
DENSE_TM = 1024
DENSE_TF = 256
DMA_UNROLL = 8
VMEM_LIMIT = 56 * 1024 * 1024


def _cparams(sem, vmem=VMEM_LIMIT):
    return pltpu.CompilerParams(dimension_semantics=sem, vmem_limit_bytes=vmem)


def _dot(a, b):
    return jnp.dot(a.astype(BF16), b.astype(BF16), preferred_element_type=F32)


def _dot_nt(a, b):
    return lax.dot_general(a.astype(BF16), b.astype(BF16), (((1,), (1,)), ((), ())),
                           preferred_element_type=F32)


def _dot_tn(a, b):
    return lax.dot_general(a.astype(BF16), b.astype(BF16), (((0,), (0,)), ((), ())),
                           preferred_element_type=F32)


def _split3(x):
    hi = x.astype(BF16)
    r1 = x - hi.astype(F32)
    mid = r1.astype(BF16)
    lo = (r1 - mid.astype(F32)).astype(BF16)
    return hi, mid, lo


def _dot_exact_lhs(m_bf16, x):
    hi, mid, lo = _split3(x)
    return (jnp.dot(m_bf16, hi, preferred_element_type=F32)
            + jnp.dot(m_bf16, mid, preferred_element_type=F32)
            + jnp.dot(m_bf16, lo, preferred_element_type=F32))


def _rms(x, w):
    ms = jnp.mean(x * x, axis=-1, keepdims=True)
    return x * lax.rsqrt(ms + NORM_EPS) * w


def _silu(x):
    return x * jax.nn.sigmoid(x)


def _norm_matmul_kernel(x_ref, nw_ref, w_ref, o_ref, xn_ref, *, col0, n, ncols):
    @pl.when(pl.program_id(1) == 0)
    def _():
        xn_ref[...] = _rms(x_ref[...], nw_ref[...]).astype(BF16)

    w = w_ref[0]
    tn = w.shape[1]
    if col0 + n > ncols:
        col = col0 + pl.program_id(1) * tn + lax.broadcasted_iota(jnp.int32, (1, tn), 1)
        w = jnp.where(col < ncols, w, 0.0)
    o_ref[...] = jnp.dot(xn_ref[...], w.astype(BF16), preferred_element_type=F32).astype(o_ref.dtype)


def norm_matmul(x, nw, w, layer, col0, n, *, tm, tn, out_dtype=F32):
    m, k = x.shape
    cb0 = col0 // tn
    return pl.pallas_call(
        functools.partial(_norm_matmul_kernel, col0=col0, n=n, ncols=w.shape[2]),
        grid=(m // tm, n // tn),
        in_specs=[pl.BlockSpec((tm, k), lambda i, j: (i, 0)),
                  pl.BlockSpec((1, k), lambda i, j: (0, 0)),
                  pl.BlockSpec((1, k, tn), lambda i, j: (layer, 0, cb0 + j))],
        out_specs=pl.BlockSpec((tm, tn), lambda i, j: (i, j)),
        out_shape=jax.ShapeDtypeStruct((m, n), out_dtype),
        scratch_shapes=[pltpu.VMEM((tm, k), BF16)],
        compiler_params=_cparams(("parallel", "arbitrary")),
        name="norm_matmul",
    )(x, nw.reshape(1, k), w)


def _chunk_masks():
    r = lax.broadcasted_iota(jnp.int32, (CHUNK, CHUNK), 0)
    c = lax.broadcasted_iota(jnp.int32, (CHUNK, CHUNK), 1)
    return r, c


def _round_robin(stages):
    live = list(stages)
    while live:
        nxt = []
        for g in live:
            try:
                next(g)
                nxt.append(g)
            except StopIteration:
                pass
        live = nxt


_STATE = "state"


def _run_chunk_pair(first, second):
    live = list(first) + list(second)
    while live:
        live = [g for g in live if next(g) != _STATE]
    _round_robin(first)
    _round_robin(second)


def _hgrn_kernel(q_ref, f_ref, i_ref, g_ref, lb_ref, nw_ref, o_ref, st_ref, *, nchunk, hpb):
    @pl.when(pl.program_id(1) == 0)
    def _():
        st_ref[...] = jnp.zeros_like(st_ref)

    nsub = CHUNK // SUB
    nw = nw_ref[...]
    r, c = _chunk_masks()
    rb = lax.shift_right_logical(r, 4)
    cb = lax.shift_right_logical(c, 4)
    mats = [(c <= r)] + [(c < SUB * (j + 1)) for j in range(nsub)]
    pmat = jnp.concatenate([m.astype(BF16) for m in mats], axis=0)
    diag_mask = (rb == cb) & (c <= r)
    rcol = lax.broadcasted_iota(jnp.int32, (CHUNK, 1), 0)
    rbcol = lax.shift_right_logical(rcol, 4)

    def head_chunk(sl, hd):
        hs = slice(hd * HEAD_DIM, (hd + 1) * HEAD_DIM)
        lb = lb_ref[:, hs]
        lbf = jnp.maximum(lb, LB_FLOOR)
        oml = 1.0 - lb
        fa = f_ref[sl, hs]
        qa = q_ref[sl, hs]
        v = i_ref[sl, hs]
        ga = g_ref[sl, hs]
        logf = jnp.log(lbf + oml * jax.nn.sigmoid(fa))
        k = oml * jax.nn.sigmoid(-fa)
        q = _silu(qa) * (HEAD_DIM ** -0.5)
        sums = _dot_exact_lhs(pmat, logf)
        yield
        cum = sums[0:CHUNK]
        ends = [sums[CHUNK * (j + 1):CHUNK * (j + 2)] for j in range(nsub)]
        last = ends[nsub - 1]
        base = jnp.zeros_like(cum)
        endv = ends[0]
        for j in range(1, nsub):
            base = jnp.where(rbcol >= j, ends[j - 1], base)
            endv = jnp.where(rbcol >= j, ends[j], endv)
        kv = _dot_tn(v, k * jnp.exp(last - cum))
        attn = jnp.where(diag_mask, _dot_nt(q * jnp.exp(cum - base), k * jnp.exp(base - cum)), 0.0)
        ko = k * jnp.exp(endv - cum)
        for j in range(nsub - 1):
            below = rbcol > j
            qo = jnp.where(below, q * jnp.exp(jnp.where(below, cum - ends[j], 0.0)), 0.0)
            kj = jnp.where(rbcol == j, ko, 0.0)
            attn = attn + _dot_nt(qo, kj)
        yield
        o = _dot(attn, v)
        qe = q * jnp.exp(cum)
        yield _STATE
        st = st_ref[hd]
        o = o + _dot_nt(qe, st)
        st_ref[hd] = st * jnp.exp(last[0:1, :]) + kv
        yield
        o_ref[sl, hs] = (_rms(o, nw) * _silu(ga)).astype(o_ref.dtype)

    def body(ci, carry):
        off = pl.multiple_of(ci * (2 * CHUNK), 2 * CHUNK)
        off2 = pl.multiple_of(off + CHUNK, CHUNK)
        _run_chunk_pair([head_chunk(pl.ds(off, CHUNK), hd) for hd in range(hpb)],
                        [head_chunk(pl.ds(off2, CHUNK), hd) for hd in range(hpb)])
        return carry

    lax.fori_loop(0, nchunk // 2, body, 0)


def hgrn_mixer(proj, lb, nw, *, tb, hpb):
    t = proj.shape[0]
    nchunk = tb // CHUNK
    ngrp = N_HEADS // hpb
    w = hpb * HEAD_DIM

    def col(part):
        return pl.BlockSpec((tb, w), lambda g, i, part=part: (i, part * ngrp + g))

    return pl.pallas_call(
        functools.partial(_hgrn_kernel, nchunk=nchunk, hpb=hpb),
        grid=(ngrp, t // tb),
        in_specs=[col(0), col(1), col(2), col(3),
                  pl.BlockSpec((1, w), lambda g, i: (0, g)),
                  pl.BlockSpec((1, HEAD_DIM), lambda g, i: (0, 0))],
        out_specs=pl.BlockSpec((tb, w), lambda g, i: (i, g)),
        out_shape=jax.ShapeDtypeStruct((t, MIX_HEADS_W), BF16),
        scratch_shapes=[pltpu.VMEM((hpb, HEAD_DIM, HEAD_DIM), F32)],
        compiler_params=_cparams(("parallel", "arbitrary")),
        name="hgrn_mixer",
    )(proj, proj, proj, proj, lb.reshape(1, MIX_HEADS_W), nw.reshape(1, HEAD_DIM))


def _lane_pick(x, lane_ids, idx):
    return jnp.sum(jnp.where(lane_ids == idx, x, 0.0), axis=1, keepdims=True)


def _gdn_kernel(q_ref, k_ref, v_ref, z_ref, ab_ref, cq_ref, ck_ref, cv_ref, alog_ref, dtb_ref, nw_ref,
                o_ref, st_ref, xq_ref, xk_ref, xv_ref, *, nchunk, tb, hpb):
    head0 = pl.program_id(0) * hpb
    tail = 8
    width = hpb * HEAD_DIM

    @pl.when(pl.program_id(1) == 0)
    def _():
        st_ref[...] = jnp.zeros_like(st_ref)
        for xr in (xq_ref, xk_ref, xv_ref):
            xr[0:tail, :] = jnp.zeros((tail, width), F32)

    for src, xr, cw in ((q_ref, xq_ref, cq_ref), (k_ref, xk_ref, ck_ref), (v_ref, xv_ref, cv_ref)):
        xr[tail:tail + tb, :] = src[...]
        acc = jnp.zeros((tb, width), F32)
        for j in range(CONV_K):
            acc = acc + cw[j:j + 1, :] * xr[pl.ds(tail - (CONV_K - 1) + j, tb), :]
        new_tail = xr[tb:tb + tail, :]
        xr[tail:tail + tb, :] = _silu(acc)
        xr[0:tail, :] = new_tail

    nw = nw_ref[...]
    r, c = _chunk_masks()
    incl = c <= r
    strict = c < r
    eye = c == r
    tri = incl.astype(BF16)
    ones = jnp.ones((CHUNK, CHUNK), BF16)
    lane = lax.broadcasted_iota(jnp.int32, (CHUNK, LANES), 1)
    neg_a = -jnp.exp(alog_ref[...])
    dtb = dtb_ref[...]

    def head_chunk(sl, slx, hd, cum_all, beta_all):
        hs = slice(hd * HEAD_DIM, (hd + 1) * HEAD_DIM)
        qc = xq_ref[slx, hs]
        kc = xk_ref[slx, hs]
        v = xv_ref[slx, hs]
        q = qc * lax.rsqrt(jnp.sum(qc * qc, axis=-1, keepdims=True) + 1e-6) * (HEAD_DIM ** -0.5)
        k = kc * lax.rsqrt(jnp.sum(kc * kc, axis=-1, keepdims=True) + 1e-6)
        cum = _lane_pick(cum_all, lane, head0 + hd)
        beta = _lane_pick(beta_all, lane, head0 + hd + N_HEADS)
        cum_b = jnp.broadcast_to(cum, (CHUNK, CHUNK))
        cum_row = _dot_exact_lhs(ones, jnp.where(eye, cum_b, 0.0))
        kk = _dot_nt(k, k)
        qk = _dot_nt(q, k)
        yield
        gam = jnp.where(incl, jnp.exp(jnp.where(incl, cum_b - cum_row, 0.0)), 0.0)
        last = cum[CHUNK - 1:CHUNK, :]
        ecum = jnp.exp(cum)
        a = jnp.where(strict, kk * gam, 0.0) * beta
        eyef = eye.astype(F32)
        tinv = eyef - a
        pw = _dot(a, a)
        yield
        for _ in range(int(math.log2(CHUNK)) - 2):
            tinv, pw = tinv + _dot(tinv, pw), _dot(pw, pw)
            yield
        tinv = tinv + _dot(tinv, pw)
        yield
        rhs = jnp.concatenate([v * beta, k * (beta * ecum)], axis=1)
        sol = _dot(tinv, rhs)
        qe = q * ecum
        qkg = qk * gam
        kdec = k * jnp.exp(last - cum)
        elast = jnp.exp(last)
        yield _STATE
        u = sol[:, :HEAD_DIM]
        w = sol[:, HEAD_DIM:]
        st = st_ref[hd]
        o = _dot(qe, st)
        v_new = u - _dot(w, st)
        yield
        o = o + _dot(qkg, v_new)
        st_ref[hd] = st * elast + _dot_tn(kdec, v_new)
        yield
        o_ref[sl, hs] = (_rms(o, nw) * _silu(z_ref[sl, hs])).astype(o_ref.dtype)

    def chunk_heads(off):
        sl = pl.ds(off, CHUNK)
        slx = pl.ds(off + tail, CHUNK)
        gates = ab_ref[sl, :]
        cum_all = _dot_exact_lhs(tri, neg_a * jax.nn.softplus(gates + dtb))
        beta_all = jax.nn.sigmoid(gates)
        return [head_chunk(sl, slx, hd, cum_all, beta_all) for hd in range(hpb)]

    def body(ci, carry):
        off = pl.multiple_of(ci * (2 * CHUNK), 2 * CHUNK)
        _run_chunk_pair(chunk_heads(off), chunk_heads(pl.multiple_of(off + CHUNK, CHUNK)))
        return carry

    lax.fori_loop(0, nchunk // 2, body, 0)


def gdn_mixer(proj, gates, gate_blk, conv_w, a_log, dt_bias, nw, *, tb, hpb):
    t = proj.shape[0]
    nchunk = tb // CHUNK
    ngrp = N_HEADS // hpb
    w = hpb * HEAD_DIM

    def col(part):
        return pl.BlockSpec((tb, w), lambda g, i, part=part: (i, part * ngrp + g))

    def cw(part):
        return pl.BlockSpec((CONV_K, w), lambda g, i, part=part: (0, part * ngrp + g))

    def row_pad(vec):
        return jnp.zeros((1, LANES), F32).at[0, :N_HEADS].set(vec.astype(F32))

    return pl.pallas_call(
        functools.partial(_gdn_kernel, nchunk=nchunk, tb=tb, hpb=hpb),
        grid=(ngrp, t // tb),
        in_specs=[col(4), col(5), col(6), col(7),
                  pl.BlockSpec((tb, LANES), lambda g, i: (i, gate_blk)),
                  cw(0), cw(1), cw(2),
                  pl.BlockSpec((1, LANES), lambda g, i: (0, 0)),
                  pl.BlockSpec((1, LANES), lambda g, i: (0, 0)),
                  pl.BlockSpec((1, HEAD_DIM), lambda g, i: (0, 0))],
        out_specs=pl.BlockSpec((tb, w), lambda g, i: (i, g)),
        out_shape=jax.ShapeDtypeStruct((t, MIX_HEADS_W), BF16),
        scratch_shapes=[pltpu.VMEM((hpb, HEAD_DIM, HEAD_DIM), F32)]
        + [pltpu.VMEM((tb + 8, w), F32) for _ in range(3)],
        compiler_params=_cparams(("parallel", "arbitrary")),
        name="gdn_mixer",
    )(proj, proj, proj, proj, gates, conv_w, conv_w, conv_w, row_pad(a_log), row_pad(dt_bias),
      nw.reshape(1, HEAD_DIM))


_S5_PW = S5_L * 2 * S5_CH
_S5_SW = 2 * S5_STATE


def _dot3(a, b):
    ah = a.astype(BF16)
    al = (a - ah.astype(F32)).astype(BF16)
    bh = b.astype(BF16)
    bl = (b - bh.astype(F32)).astype(BF16)
    return (jnp.dot(ah, bh, preferred_element_type=F32) + jnp.dot(ah, bl, preferred_element_type=F32)
            + jnp.dot(al, bh, preferred_element_type=F32))


def _s5_tables_kernel(arc_ref, aic_ref, lsc_ref, arr_ref, air_ref, lsr_ref, cre_ref, cim_ref, bre_ref, bim_ref,
                      wcat_ref, ws2o_ref, alr_ref, ali_ref):
    are = jnp.minimum(arc_ref[0], -1e-4)
    aim = aic_ref[0]
    delta = jnp.exp(lsc_ref[0])
    lane = lax.broadcasted_iota(jnp.int32, (1, _S5_PW), 1)
    tau = lax.shift_right_logical(lane, 5).astype(F32)
    mag = jnp.exp(are * delta * tau)
    ang = aim * delta * tau
    lr = mag * jnp.cos(ang)
    li = mag * jnp.sin(ang)
    cre = cre_ref[0]
    cim = cim_ref[0]
    rr0 = jnp.concatenate([lr * cre - li * cim, -(lr * cim + li * cre)], axis=0)
    m1 = jnp.exp(are * delta)
    l1r = m1 * jnp.cos(aim * delta)
    l1i = m1 * jnp.sin(aim * delta)
    lr1 = lr * l1r - li * l1i
    li1 = lr * l1i + li * l1r
    ws2o_ref[0] = jnp.concatenate([lr1 * cre - li1 * cim, -(lr1 * cim + li1 * cre)], axis=0).astype(BF16)

    ar = jnp.minimum(arr_ref[0], -1e-4)
    ai = air_ref[0]
    dl = jnp.exp(lsr_ref[0])
    mb = jnp.exp(ar * dl)
    lbr = mb * jnp.cos(ai * dl)
    lbi = mb * jnp.sin(ai * dl)
    den = ar * ar + ai * ai
    xr = lbr - 1.0
    cr = (xr * ar + lbi * ai) / den
    ci = (lbi * ar - xr * ai) / den
    bre = bre_ref[0]
    bim = bim_ref[0]
    bbr = cr * bre - ci * bim
    bbi = cr * bim + ci * bre
    g = _dot3(jnp.concatenate([bbr, bbi], axis=1), rr0)
    rows_per = 2 * S5_CH
    for s in range(S5_L):
        if s == 0:
            blk = g
        else:
            blk = jnp.where(lane >= rows_per * s, pltpu.roll(g, rows_per * s, axis=1), 0.0)
        wcat_ref[0, rows_per * s:rows_per * (s + 1), 0:_S5_PW] = blk.astype(BF16)
    row = lax.broadcasted_iota(jnp.int32, (_S5_PW, 1), 0)
    mpow = (S5_L - 1 - lax.shift_right_logical(row, 5)).astype(F32)
    pm = jnp.exp(ar * dl * mpow)
    pr = pm * jnp.cos(ai * dl * mpow)
    pi = pm * jnp.sin(ai * dl * mpow)
    bbr_t = jnp.concatenate([bbr] * S5_L, axis=0)
    bbi_t = jnp.concatenate([bbi] * S5_L, axis=0)
    wcat_ref[0, :, _S5_PW:_S5_PW + _S5_SW] = (pr * bbr_t - pi * bbi_t).astype(BF16)
    wcat_ref[0, :, _S5_PW + _S5_SW:] = (pr * bbi_t + pi * bbr_t).astype(BF16)
    ml = jnp.exp(ar * dl * S5_L)
    alr_ref[0] = ml * jnp.cos(ai * dl * S5_L)
    ali_ref[0] = ml * jnp.sin(ai * dl * S5_L)


def s5_tables(a_re, a_im, b_re, b_im, c_re, c_im, log_step):
    p = S5_PAIRS
    ls = jnp.broadcast_to(log_step.astype(F32)[:, None], (S5_GROUPS, S5_STATE))
    same = (jnp.eye(2, dtype=F32) > 0).reshape(1, 2, 2, 1, 1)

    def ctile(cm):
        x = jnp.where(same, cm.astype(F32).reshape(p, 1, 2, S5_CH, S5_STATE), 0.0)
        x = x.transpose(0, 1, 4, 2, 3)[:, :, :, None]
        return jnp.broadcast_to(x, (p, 2, S5_STATE, S5_L, 2, S5_CH)).reshape(p, _S5_SW, _S5_PW)

    def btile(bm):
        x = jnp.where(same, bm.astype(F32).reshape(p, 2, 1, S5_STATE, S5_CH), 0.0)
        return x.transpose(0, 1, 4, 2, 3).reshape(p, 2 * S5_CH, _S5_SW)

    col = lambda v: v.astype(F32).reshape(p, _S5_SW, 1)
    rowv = lambda v: v.astype(F32).reshape(p, 1, _S5_SW)
    spec = lambda shape: pl.BlockSpec((1,) + shape, lambda i: (i, 0, 0))
    return pl.pallas_call(
        _s5_tables_kernel,
        grid=(p,),
        in_specs=[spec((_S5_SW, 1))] * 3 + [spec((1, _S5_SW))] * 3
        + [spec((_S5_SW, _S5_PW))] * 2 + [spec((2 * S5_CH, _S5_SW))] * 2,
        out_specs=[spec((_S5_PW, _S5_PW + 2 * _S5_SW)), spec((2 * _S5_SW, _S5_PW)),
                   spec((1, _S5_SW)), spec((1, _S5_SW))],
        out_shape=[jax.ShapeDtypeStruct((p, _S5_PW, _S5_PW + 2 * _S5_SW), BF16),
                   jax.ShapeDtypeStruct((p, 2 * _S5_SW, _S5_PW), BF16),
                   jax.ShapeDtypeStruct((p, 1, _S5_SW), F32),
                   jax.ShapeDtypeStruct((p, 1, _S5_SW), F32)],
        compiler_params=_cparams(("parallel",)),
        name="s5_tables",
    )(col(a_re), col(a_im), col(ls), rowv(a_re), rowv(a_im), rowv(ls),
      ctile(c_re), ctile(c_im), btile(b_re), btile(b_im))


def _s5_in_kernel(u_ref, wcat_ref, d_ref, yi_ref, vre_ref, vim_ref):
    u = u_ref[0]
    ycat = jnp.dot(u.astype(BF16), wcat_ref[0], preferred_element_type=F32)
    yi_ref[0] = ycat[:, :_S5_PW] + d_ref[0] * u
    vre_ref[...] = ycat[:, _S5_PW:_S5_PW + _S5_SW]
    vim_ref[...] = ycat[:, _S5_PW + _S5_SW:]


def _s5_scan_kernel(vre_ref, vim_ref, alr_ref, ali_ref, xre_ref, xim_ref, *, nsteps):
    ar = alr_ref[...]
    ai = ali_ref[...]

    def body(j, carry):
        xr, xi = carry
        xre_ref[j] = xr
        xim_ref[j] = xi
        return (ar * xr - ai * xi + vre_ref[j], ar * xi + ai * xr + vim_ref[j])

    zero = jnp.zeros(ar.shape, F32)
    lax.fori_loop(0, nsteps, body, (zero, zero))


def _gelu_tanh(x):
    return 0.5 * x * (1.0 + jnp.tanh(math.sqrt(2.0 / math.pi) * (x + 0.044715 * (x * x * x))))


def _s5_out_kernel(yi_ref, xre_ref, xim_ref, ws2o_ref, y_ref):
    w = ws2o_ref[0]
    y = (yi_ref[0] + jnp.dot(xre_ref[...].astype(BF16), w[:_S5_SW], preferred_element_type=F32)
         + jnp.dot(xim_ref[...].astype(BF16), w[_S5_SW:], preferred_element_type=F32))
    y_ref[0] = _gelu_tanh(y)


def _glu_kernel(y_ref, w_ref, b_ref, o_ref):
    y = y_ref[...]
    z = jnp.dot(y.astype(BF16), w_ref[...], preferred_element_type=F32) + b_ref[...]
    o_ref[...] = (y * jax.nn.sigmoid(z)).astype(o_ref.dtype)


def s5_mixer(u, a_re, a_im, b_re, b_im, c_re, c_im, d_skip, log_step, w_glu, b_glu):
    t = u.shape[0]
    nj = t // S5_L
    p = S5_PAIRS
    wcat, ws2o, alr, ali = s5_tables(a_re, a_im, b_re, b_im, c_re, c_im, log_step)
    up = u.reshape(nj, S5_L, p, 2 * S5_CH).transpose(2, 0, 1, 3).reshape(p, nj, _S5_PW)
    dt = jnp.tile(d_skip.astype(F32).reshape(p, 1, 2 * S5_CH), (1, 1, S5_L))
    pair3 = lambda shape: pl.BlockSpec((1,) + shape, lambda i: (i, 0, 0))
    lanes = lambda: pl.BlockSpec((nj, _S5_SW), lambda i: (0, i))
    yi, vre, vim = pl.pallas_call(
        _s5_in_kernel,
        grid=(p,),
        in_specs=[pair3((nj, _S5_PW)), pair3((_S5_PW, _S5_PW + 2 * _S5_SW)), pair3((1, _S5_PW))],
        out_specs=[pair3((nj, _S5_PW)), lanes(), lanes()],
        out_shape=[jax.ShapeDtypeStruct((p, nj, _S5_PW), F32),
                   jax.ShapeDtypeStruct((nj, p * _S5_SW), F32),
                   jax.ShapeDtypeStruct((nj, p * _S5_SW), F32)],
        compiler_params=_cparams(("parallel",)),
        name="s5_in",
    )(up, wcat, dt)
    sw = p * _S5_SW
    tile = (8, sw // 8)
    lb = tile[1] // 2
    sblk = lambda: pl.BlockSpec((nj, 8, lb), lambda i: (0, 0, i))
    ablk = lambda: pl.BlockSpec((8, lb), lambda i: (0, i))
    xre, xim = pl.pallas_call(
        functools.partial(_s5_scan_kernel, nsteps=nj),
        grid=(2,),
        in_specs=[sblk(), sblk(), ablk(), ablk()],
        out_specs=[sblk(), sblk()],
        out_shape=[jax.ShapeDtypeStruct((nj,) + tile, F32)] * 2,
        compiler_params=_cparams(("parallel",)),
        name="s5_scan",
    )(vre.reshape((nj,) + tile), vim.reshape((nj,) + tile), alr.reshape(tile), ali.reshape(tile))
    yp = pl.pallas_call(
        _s5_out_kernel,
        grid=(p,),
        in_specs=[pair3((nj, _S5_PW)), lanes(), lanes(), pair3((2 * _S5_SW, _S5_PW))],
        out_specs=pair3((nj, _S5_PW)),
        out_shape=jax.ShapeDtypeStruct((p, nj, _S5_PW), F32),
        compiler_params=_cparams(("parallel",)),
        name="s5_out",
    )(yi, xre.reshape(nj, sw), xim.reshape(nj, sw), ws2o)
    y = yp.reshape(p, nj, S5_L, 2 * S5_CH).transpose(1, 2, 0, 3).reshape(t, S5_W)
    tm = min(t, 1024)
    return pl.pallas_call(
        _glu_kernel,
        grid=(t // tm,),
        in_specs=[pl.BlockSpec((tm, S5_W), lambda i: (i, 0)),
                  pl.BlockSpec((S5_W, S5_W), lambda i: (0, 0)),
                  pl.BlockSpec((1, S5_W), lambda i: (0, 0))],
        out_specs=pl.BlockSpec((tm, S5_W), lambda i: (i, 0)),
        out_shape=jax.ShapeDtypeStruct((t, S5_W), BF16),
        compiler_params=_cparams(("parallel",)),
        name="s5_glu",
    )(y, w_glu.astype(BF16), b_glu.astype(F32).reshape(1, S5_W))


def _lower_bounds_kernel(x_ref, o_ref):
    x = x_ref[...]
    e = jnp.exp(x - jnp.max(x, axis=0, keepdims=True))
    p = e / jnp.sum(e, axis=0, keepdims=True)
    run = p[0:1]
    rows = [run - p[0:1]]
    for i in range(1, x.shape[0]):
        run = run + p[i:i + 1]
        rows.append(run - p[0:1])
    o_ref[...] = jnp.concatenate(rows, axis=0)


def lower_bounds(logits):
    return pl.pallas_call(
        _lower_bounds_kernel,
        out_shape=jax.ShapeDtypeStruct(logits.shape, F32),
        name="hgrn_lower_bounds",
    )(logits.astype(F32))


def _out_proj_kernel(h_ref, oa_ref, ob_ref, yc_ref, w1_ref, w2_ref, w3_ref, o_ref):
    o_ref[...] = (h_ref[...]
                  + jnp.dot(oa_ref[...], w1_ref[0].astype(BF16), preferred_element_type=F32)
                  + jnp.dot(ob_ref[...], w2_ref[0].astype(BF16), preferred_element_type=F32)
                  + jnp.dot(yc_ref[...], w3_ref[0].astype(BF16), preferred_element_type=F32))


def out_proj(h, oa, ob, yc, w, layer, *, tm, tn):
    m, n = h.shape
    hw = MIX_HEADS_W
    return pl.pallas_call(
        _out_proj_kernel,
        grid=(m // tm, n // tn),
        in_specs=[pl.BlockSpec((tm, tn), lambda i, j: (i, j)),
                  pl.BlockSpec((tm, hw), lambda i, j: (i, 0)),
                  pl.BlockSpec((tm, hw), lambda i, j: (i, 0)),
                  pl.BlockSpec((tm, S5_W), lambda i, j: (i, 0)),
                  pl.BlockSpec((1, hw, tn), lambda i, j: (layer, 0, j)),
                  pl.BlockSpec((1, hw, tn), lambda i, j: (layer, 1, j)),
                  pl.BlockSpec((1, S5_W, tn), lambda i, j: (layer, 2 * hw // S5_W, j))],
        out_specs=pl.BlockSpec((tm, tn), lambda i, j: (i, j)),
        out_shape=jax.ShapeDtypeStruct((m, n), F32),
        compiler_params=_cparams(("parallel", "arbitrary")),
        name="out_proj",
    )(h, oa, ob, yc, w, w, w)


def _xattn_kernel(h_ref, nw_ref, wq_ref, kv_ref, wo_ref, o_ref, wqb_ref, wob_ref):
    @pl.when(pl.program_id(0) == 0)
    def _():
        wqb_ref[...] = wq_ref[0].astype(BF16)
        wob_ref[...] = wo_ref[0].astype(BF16)

    h = h_ref[...]
    q = jnp.dot(_rms(h, nw_ref[...]).astype(BF16), wqb_ref[...], preferred_element_type=F32)
    kv = kv_ref[...]
    outs = []
    for hd in range(XA_HEADS):
        lo, hi = hd * XA_DH, (hd + 1) * XA_DH
        s = _dot_nt(q[:, lo:hi], kv[:, lo:hi]) * (XA_DH ** -0.5)
        e = jnp.exp(s - jnp.max(s, axis=-1, keepdims=True))
        p = e / jnp.sum(e, axis=-1, keepdims=True)
        outs.append(_dot(p, kv[:, XA_W + lo:XA_W + hi]))
    o = jnp.concatenate(outs, axis=1)
    o_ref[...] = h + _dot(o, wob_ref[...])


def xattn(h, nw, wq, kv, wo, layer, *, tm):
    m, d = h.shape
    nw = nw.reshape(1, d)
    return pl.pallas_call(
        _xattn_kernel,
        grid=(m // tm,),
        in_specs=[pl.BlockSpec((tm, d), lambda i: (i, 0)),
                  pl.BlockSpec((1, d), lambda i: (0, 0)),
                  pl.BlockSpec((1, d, XA_W), lambda i: (layer, 0, 0)),
                  pl.BlockSpec(kv.shape, lambda i: (0, 0)),
                  pl.BlockSpec((1, XA_W, d), lambda i: (layer, 0, 0))],
        out_specs=pl.BlockSpec((tm, d), lambda i: (i, 0)),
        out_shape=jax.ShapeDtypeStruct((m, d), F32),
        scratch_shapes=[pltpu.VMEM((d, XA_W), BF16), pltpu.VMEM((XA_W, d), BF16)],
        compiler_params=_cparams(("arbitrary",)),
        name="xattn",
    )(h, nw, wq, kv, wo)


def _ffn_kernel(te_ref, nv_ref, x_ref, nw_ref, wg_ref, wu_ref, wd_ref, o_ref, xn_ref, *, norm, residual, nf, lead):
    i = pl.program_id(0)
    f = pl.program_id(1)
    widx = (0,) * (lead + 1)

    @pl.when(i < nv_ref[0])
    def _():
        @pl.when(f == 0)
        def _():
            x = x_ref[...]
            xn_ref[...] = (_rms(x, nw_ref[...]) if norm else x).astype(BF16)
            o_ref[...] = x if residual else jnp.zeros_like(o_ref)

        xn = xn_ref[...]
        g = jnp.dot(xn, wg_ref[widx].astype(BF16), preferred_element_type=F32)
        u = jnp.dot(xn, wu_ref[widx].astype(BF16), preferred_element_type=F32)
        o_ref[...] += jnp.dot((_silu(g) * u).astype(BF16), wd_ref[widx].astype(BF16), preferred_element_type=F32)

    @pl.when((i >= nv_ref[0]) & (f == nf - 1))
    def _():
        o_ref[...] = jnp.zeros_like(o_ref)


def ffn(x, nw, wg, wu, wd, lead, tile_expert, nvalid, *, tm, tf, norm, residual):
    m, d = x.shape
    dff = wg.shape[-1]
    nf = dff // tf
    lead = tuple(lead)
    ones = (1,) * (len(lead) + 1)

    def row(i, f, te, nv):
        return (jnp.minimum(i, nv[0] - 1), 0)

    def wcol(i, f, te, nv):
        return lead + (te[jnp.minimum(i, nv[0] - 1)], 0, jnp.where(i < nv[0], f, nf - 1))

    def wrow(i, f, te, nv):
        return lead + (te[jnp.minimum(i, nv[0] - 1)], jnp.where(i < nv[0], f, nf - 1), 0)

    return pl.pallas_call(
        functools.partial(_ffn_kernel, norm=norm, residual=residual, nf=nf, lead=len(lead)),
        grid_spec=pltpu.PrefetchScalarGridSpec(
            num_scalar_prefetch=2,
            grid=(m // tm, nf),
            in_specs=[pl.BlockSpec((tm, d), row),
                      pl.BlockSpec((1, d), lambda i, f, te, nv: (0, 0)),
                      pl.BlockSpec(ones + (d, tf), wcol),
                      pl.BlockSpec(ones + (d, tf), wcol),
                      pl.BlockSpec(ones + (tf, d), wrow)],
            out_specs=pl.BlockSpec((tm, d), lambda i, f, te, nv: (i, 0)),
            scratch_shapes=[pltpu.VMEM((tm, d), BF16)]),
        out_shape=jax.ShapeDtypeStruct((m, d), F32),
        compiler_params=_cparams(("arbitrary", "arbitrary")),
        name="ffn",
    )(tile_expert, nvalid, x, nw.reshape(1, d), wg, wu, wd)


def _router_kernel(h_ref, nw_ref, rw_ref, hn_ref, info_ref, cnt_ref, carry_ref, tri_ref, *, tm):
    @pl.when(pl.program_id(0) == 0)
    def _():
        carry_ref[...] = jnp.zeros_like(carry_ref)
        r = lax.broadcasted_iota(jnp.int32, (tm, tm), 0)
        c = lax.broadcasted_iota(jnp.int32, (tm, tm), 1)
        tri_ref[...] = (c < r).astype(BF16)

    hn = _rms(h_ref[...], nw_ref[...])
    hn_ref[...] = hn
    logits = _dot3(hn, rw_ref[...])
    lane = lax.broadcasted_iota(jnp.int32, (tm, LANES), 1).astype(F32)
    neg = -jnp.inf
    lg = jnp.where(lane < N_EXPERTS, logits, neg)
    m1 = jnp.max(lg, axis=1, keepdims=True)
    i1 = jnp.min(jnp.where(lg == m1, lane, float(LANES)), axis=1, keepdims=True)
    lg2 = jnp.where(lane == i1, neg, lg)
    m2 = jnp.max(lg2, axis=1, keepdims=True)
    i2 = jnp.min(jnp.where(lg2 == m2, lane, float(LANES)), axis=1, keepdims=True)
    e2 = jnp.exp(m2 - m1)
    g1 = 1.0 / (1.0 + e2)
    g2 = e2 / (1.0 + e2)
    cnt = jnp.where((lane == i1) | (lane == i2), 1.0, 0.0)
    carry = carry_ref[...]
    before = jnp.dot(tri_ref[...], cnt.astype(BF16), preferred_element_type=F32) + carry
    r1 = jnp.sum(jnp.where(lane == i1, before, 0.0), axis=1, keepdims=True)
    r2 = jnp.sum(jnp.where(lane == i2, before, 0.0), axis=1, keepdims=True)
    carry = carry + jnp.sum(cnt, axis=0, keepdims=True)
    carry_ref[...] = carry
    cnt_ref[...] = jnp.broadcast_to(carry, cnt_ref.shape)
    info = jnp.zeros((tm, LANES), F32)
    for ln, val in enumerate((i1, i2, r1, r2, g1, g2)):
        info = jnp.where(lane == ln, val, info)
    info_ref[...] = info


def router(h, nw, rw, *, tm):
    m, d = h.shape
    rwp = jnp.zeros((d, LANES), F32).at[:, :N_EXPERTS].set(rw.astype(F32))
    return pl.pallas_call(
        functools.partial(_router_kernel, tm=tm),
        grid=(m // tm,),
        in_specs=[pl.BlockSpec((tm, d), lambda i: (i, 0)),
                  pl.BlockSpec((1, d), lambda i: (0, 0)),
                  pl.BlockSpec((d, LANES), lambda i: (0, 0))],
        out_specs=[pl.BlockSpec((tm, d), lambda i: (i, 0)),
                   pl.BlockSpec((tm, LANES), lambda i: (i, 0)),
                   pl.BlockSpec((8, LANES), lambda i: (0, 0))],
        out_shape=[jax.ShapeDtypeStruct((m, d), F32),
                   jax.ShapeDtypeStruct((m, LANES), F32),
                   jax.ShapeDtypeStruct((8, LANES), F32)],
        scratch_shapes=[pltpu.VMEM((1, LANES), F32), pltpu.VMEM((tm, tm), BF16)],
        compiler_params=_cparams(("arbitrary",)),
        name="moe_router",
    )(h, nw.reshape(1, d), rwp)


def _dispatch_kernel(dest_ref, hn_ref, init_ref, xb_ref, sem, *, tm):
    del init_ref
    base = pl.program_id(0) * tm

    def row_copy(t, d):
        return pltpu.make_async_copy(hn_ref.at[pl.ds(t, 1)], xb_ref.at[pl.ds(d, 1)], sem)

    def issue(t, c):
        row_copy(t, dest_ref[2 * (base + t)]).start()
        row_copy(t, dest_ref[2 * (base + t) + 1]).start(priority=1)
        return c

    def drain(t, c):
        row_copy(0, 0).wait()
        row_copy(0, 0).wait()
        return c

    lax.fori_loop(0, tm, issue, 0, unroll=DMA_UNROLL)
    lax.fori_loop(0, tm, drain, 0, unroll=DMA_UNROLL)


def dispatch(hn, dest, rows, *, tm):
    m, d = hn.shape
    return pl.pallas_call(
        functools.partial(_dispatch_kernel, tm=tm),
        grid_spec=pltpu.PrefetchScalarGridSpec(
            num_scalar_prefetch=1,
            grid=(m // tm,),
            in_specs=[pl.BlockSpec((tm, d), lambda i, dst: (i, 0)),
                      pl.BlockSpec(memory_space=pl.ANY)],
            out_specs=pl.BlockSpec(memory_space=pl.ANY),
            scratch_shapes=[pltpu.SemaphoreType.DMA(())]),
        out_shape=jax.ShapeDtypeStruct((rows, d), F32),
        input_output_aliases={2: 0},
        compiler_params=_cparams(("arbitrary",)),
        name="moe_dispatch",
    )(dest, hn, jnp.zeros((rows, d), F32))


def _combine_kernel(dest_ref, h_ref, info_ref, yb_ref, o_ref, buf_ref, sem, *, tm):
    i = pl.program_id(0)
    n = pl.num_programs(0)

    def row_copy(t, d, slot, k):
        return pltpu.make_async_copy(yb_ref.at[pl.ds(d, 1)], buf_ref.at[slot, k, pl.ds(t, 1)], sem.at[slot])

    def gather(tile, slot):
        base = tile * tm

        def issue(t, c):
            row_copy(t, dest_ref[2 * (base + t)], slot, 0).start()
            row_copy(t, dest_ref[2 * (base + t) + 1], slot, 1).start(priority=1)
            return c

        lax.fori_loop(0, tm, issue, 0, unroll=DMA_UNROLL)

    @pl.when(i == 0)
    def _():
        gather(0, 0)

    @pl.when(i + 1 < n)
    def _():
        gather(i + 1, (i + 1) % 2)

    slot = i % 2

    def drain(t, c):
        row_copy(0, 0, slot, 0).wait()
        row_copy(0, 0, slot, 1).wait()
        return c

    lax.fori_loop(0, tm, drain, 0, unroll=DMA_UNROLL)
    info = info_ref[...]
    o_ref[...] = h_ref[...] + (info[:, 4:5] * buf_ref[slot, 0] + info[:, 5:6] * buf_ref[slot, 1])


def combine(h, info, yb, dest, *, tm):
    m, d = h.shape
    return pl.pallas_call(
        functools.partial(_combine_kernel, tm=tm),
        grid_spec=pltpu.PrefetchScalarGridSpec(
            num_scalar_prefetch=1,
            grid=(m // tm,),
            in_specs=[pl.BlockSpec((tm, d), lambda i, dst: (i, 0)),
                      pl.BlockSpec((tm, LANES), lambda i, dst: (i, 0)),
                      pl.BlockSpec(memory_space=pl.ANY)],
            out_specs=pl.BlockSpec((tm, d), lambda i, dst: (i, 0)),
            scratch_shapes=[pltpu.VMEM((2, 2, tm, d), F32), pltpu.SemaphoreType.DMA((2,))]),
        out_shape=jax.ShapeDtypeStruct((m, d), F32),
        compiler_params=_cparams(("arbitrary",)),
        name="moe_combine",
    )(dest, h, info, yb)


def moe_layer(h, nw, rw, wg, wu, wd, lead):
    m, d = h.shape
    tm = MOE_TM
    ntiles = m * 2 // tm + N_EXPERTS
    hn, info, cnt = router(h, nw, rw, tm=min(m, ROW_TM))
    counts = cnt[0, :N_EXPERTS].astype(jnp.int32)
    seg = (counts + tm - 1) // tm * tm
    seg_end = jnp.cumsum(seg)
    seg_start = seg_end - seg
    ids = info[:, 0:2].astype(jnp.int32)
    dest = (seg_start[ids] + info[:, 2:4].astype(jnp.int32)).reshape(2 * m)
    tile_start = jnp.arange(ntiles, dtype=jnp.int32) * tm
    tile_expert = jnp.minimum(
        jnp.sum((seg_end[None, :] <= tile_start[:, None]).astype(jnp.int32), axis=1), N_EXPERTS - 1)
    nvalid = (seg_end[-1:] // tm).astype(jnp.int32)
    xb = dispatch(hn, dest, ntiles * tm, tm=256)
    yb = ffn(xb, nw, wg, wu, wd, lead, tile_expert, nvalid, tm=tm, tf=MOE_TF, norm=False, residual=False)
    return combine(h, info, yb, dest, tm=256)


def _final_norm_kernel(x_ref, w_ref, o_ref):
    o_ref[...] = _rms(x_ref[...], w_ref[...])


def final_norm(h, w, *, tm):
    m, d = h.shape
    return pl.pallas_call(
        _final_norm_kernel,
        grid=(m // tm,),
        in_specs=[pl.BlockSpec((tm, d), lambda i: (i, 0)), pl.BlockSpec((1, d), lambda i: (0, 0))],
        out_specs=pl.BlockSpec((tm, d), lambda i: (i, 0)),
        out_shape=jax.ShapeDtypeStruct((m, d), F32),
        compiler_params=_cparams(("parallel",)),
        name="final_norm",
    )(h, w.reshape(1, d))


def kernel(x, mem, norm_mix, w_in, w_out, hgrn_lb_logits, hgrn_norm, gdn_conv, gdn_a_log, gdn_dt_bias, gdn_norm, s5_a_re, s5_a_im, s5_b_re, s5_b_im, s5_c_re, s5_c_im, s5_d, s5_log_step, s5_w_glu, s5_b_glu, norm_cross, norm_mem, xa_wq, xa_wk, xa_wv, xa_wo, norm_ffn, ffn_w_gate, ffn_w_up, ffn_w_down, moe_router, moe_w_gate, moe_w_up, moe_w_down, norm_final):
    bsz, t, d = x.shape
    assert bsz == 1
    depth = w_in.shape[0]
    h = x[0].astype(F32)
    memx = mem[0].astype(F32)
    lbs = lower_bounds(hgrn_lb_logits)
    tb = min(t, 512)
    tmr = min(t, 1024)
    dense_tm = min(t, DENSE_TM)
    dense_tiles = t // dense_tm
    w_kv = jnp.concatenate([xa_wk, xa_wv], axis=2)
    ngate = 2 * N_HEADS
    for layer in range(depth):
        proj = norm_matmul(h, norm_mix[layer], w_in, layer, 0, IN_WIDE, tm=tmr, tn=512)
        rest = norm_matmul(h, norm_mix[layer], w_in, layer, IN_WIDE, LANES, tm=tmr, tn=LANES)
        oa = hgrn_mixer(proj, lbs[layer], hgrn_norm[layer], tb=tb, hpb=HEADS_PER_BODY)
        ob = gdn_mixer(proj, proj, IN_MAIN // LANES, gdn_conv[layer].astype(F32), gdn_a_log[layer],
                       gdn_dt_bias[layer], gdn_norm[layer], tb=tb, hpb=HEADS_PER_BODY)
        u_s5 = jnp.concatenate([proj[:, IN_MAIN + ngate:], rest[:, :IN_MAIN + ngate + S5_W - IN_WIDE]], axis=1)
        yc = s5_mixer(u_s5, s5_a_re[layer], s5_a_im[layer], s5_b_re[layer],
                      s5_b_im[layer], s5_c_re[layer], s5_c_im[layer], s5_d[layer], s5_log_step[layer],
                      s5_w_glu[layer], s5_b_glu[layer])
        h = out_proj(h, oa, ob, yc, w_out, layer, tm=tmr, tn=512)
        kv = norm_matmul(memx, norm_mem[layer], w_kv, layer, 0, 2 * XA_W, tm=memx.shape[0], tn=512,
                         out_dtype=BF16)
        h = xattn(h, norm_cross[layer], xa_wq, kv, xa_wo, layer, tm=min(t, ROW_TM))
        i = layer // 2
        if layer % 2 == 0:
            h = ffn(h, norm_ffn[layer], ffn_w_gate, ffn_w_up, ffn_w_down, (), jnp.full((dense_tiles,), i, jnp.int32),
                    jnp.full((1,), dense_tiles, jnp.int32), tm=dense_tm, tf=DENSE_TF, norm=True, residual=True)
        else:
            h = moe_layer(h, norm_ffn[layer], moe_router[i], moe_w_gate, moe_w_up, moe_w_down, (i,))
    return final_norm(h, norm_final, tm=tmr)[None].astype(x.dtype)
```
